```python
import jax, jax.numpy as jnp
from jax import lax
import numpy as np

D_MODEL = 2048
BATCH = 32
SEQ = 256
DEPTH = 4
DEC_BATCH = 4
DEC_SEQ = 4096
PAST_LEN = 512

GRID_W = 64
N_EVEN = (DEPTH + 1) // 2
N_ODD = DEPTH // 2
N_MOD = 6
EPS = 1e-6
D_A = D_MODEL
A_GROUPS = 8
A_CH = D_A // A_GROUPS
CHUNK = 128
D_B = D_MODEL
SSD_HEAD_DIM = 64
SSD_HEADS = D_B // SSD_HEAD_DIM
SSD_GROUPS = 4
D_STATE = 128
D_CONV = 5
SSD_CHUNK = 128
C_CONV = D_B + 2 * SSD_GROUPS * D_STATE
IN_EVEN = 2 * D_A + D_B + C_CONV + 2 * SSD_HEADS
D_C = D_MODEL
POOL_WINDOWS = (2, 4, 8, 16)
POOL_CH = D_C // len(POOL_WINDOWS)
D_FF = -(-8 * D_MODEL // (3 * 256)) * 256

kernel_name = "hybrid_gmlp_ssd_pool_diffusion_step"

F32 = jnp.float32


def rmsnorm(x, w):
    xf = x.astype(F32)
    y = xf * lax.rsqrt(jnp.mean(xf * xf, axis=-1, keepdims=True) + EPS)
    return (y * w.astype(F32)).astype(x.dtype)


def modulate(h, shift, scale):
    return h * (1 + scale) + shift


def chunk_gmlp(u, v, ws, bs, vnorm):
    b, l, _ = u.shape
    nc = l // CHUNK
    u = jax.nn.gelu(u)
    v = jax.nn.gelu(v).reshape(b, nc, CHUNK, A_GROUPS, A_CH)
    v = rmsnorm(v, vnorm.reshape(A_GROUPS, A_CH))
    sv = jnp.einsum('gts,bcsgk->bctgk', ws, v) + bs.T[:, :, None]
    return u * sv.reshape(b, l, D_A)


def centred_conv(x, w, bias):
    l = x.shape[1]
    pad = D_CONV // 2
    xp = jnp.pad(x, ((0, 0), (pad, pad), (0, 0)))
    out = bias
    for k in range(D_CONV):
        out = out + w[k] * xp[:, k:k + l]
    return out


def segsum(a):
    t = a.shape[-1]
    cs = jnp.cumsum(a, axis=-1)
    diff = cs[..., :, None] - cs[..., None, :]
    mask = jnp.tril(jnp.ones((t, t), dtype=bool))
    return jnp.where(mask, diff, -jnp.inf)


def ssd_scan(x, dt, a_log, bm, cm, h0):
    b, l, h, p = x.shape
    q = SSD_CHUNK
    nc = l // q
    j = h // SSD_GROUPS
    dtf = dt.astype(F32)
    a = dtf * (-jnp.exp(a_log.astype(F32)))
    a = a.reshape(b, nc, q, SSD_GROUPS, j).transpose(0, 3, 4, 1, 2)
    a_cs = jnp.cumsum(a, axis=-1)
    xg = (x.astype(F32) * dtf[..., None]).reshape(b, nc, q, SSD_GROUPS, j, p)
    bc = bm.astype(F32).reshape(b, nc, q, SSD_GROUPS, D_STATE)
    cc = cm.astype(F32).reshape(b, nc, q, SSD_GROUPS, D_STATE)
    cb = jnp.einsum('bclgn,bcsgn->bgcls', cc, bc)
    m = cb[:, :, None] * jnp.exp(segsum(a))
    y_diag = jnp.einsum('bgjcls,bcsgjp->bclgjp', m, xg)
    decay_states = jnp.exp(a_cs[..., -1:] - a_cs)
    states = jnp.einsum('bcsgn,bgjcs,bcsgjp->bcgjpn', bc, decay_states, xg)
    init = h0.astype(F32).reshape(b, 1, SSD_GROUPS, j, p, D_STATE)
    states = jnp.concatenate([init, states], axis=1)
    chunk_tot = jnp.pad(a_cs[..., -1], ((0, 0), (0, 0), (0, 0), (1, 0)))
    decay_chunk = jnp.exp(segsum(chunk_tot))
    new_states = jnp.einsum('bgjzc,bcgjpn->bzgjpn', decay_chunk, states)
    y_off = jnp.einsum('bclgn,bcgjpn,bgjcl->bclgjp', cc, new_states[:, :-1], jnp.exp(a_cs))
    y = (y_diag + y_off).reshape(b, l, h, p)
    final = new_states[:, -1].reshape(b, h, p, D_STATE)
    return y.astype(x.dtype), final


def ssd_mixer(z, xbc, dt_f, dt_b, conv_w, conv_b, dt_bias, a_log, d_skip, norm_w, h0):
    b, l, _ = z.shape
    gn = SSD_GROUPS * D_STATE
    xbc = jax.nn.silu(centred_conv(xbc, conv_w, conv_b))
    xs = xbc[..., :D_B].reshape(b, l, SSD_HEADS, SSD_HEAD_DIM)
    bm = xbc[..., D_B:D_B + gn].reshape(b, l, SSD_GROUPS, D_STATE)
    cm = xbc[..., D_B + gn:].reshape(b, l, SSD_GROUPS, D_STATE)
    dtf = jax.nn.softplus(dt_f.astype(F32) + dt_bias[0].astype(F32))
    dtb = jax.nn.softplus(dt_b.astype(F32) + dt_bias[1].astype(F32))
    y_f, s_f = ssd_scan(xs, dtf, a_log[0], bm, cm, h0[:, 0])
    y_b, s_b = ssd_scan(jnp.flip(xs, 1), jnp.flip(dtb, 1), a_log[1],
                        jnp.flip(bm, 1), jnp.flip(cm, 1), h0[:, 1])
    y = y_f + jnp.flip(y_b, 1) + d_skip[:, None] * xs
    y = rmsnorm(y.reshape(b, l, D_B) * jax.nn.silu(z), norm_w)
    return y, jnp.stack([s_f, s_b], axis=1)


def window_mean(x4, k):
    w = x4.shape[2]
    s = jnp.pad(jnp.cumsum(x4.astype(F32), axis=2), ((0, 0), (0, 0), (1, 0), (0, 0)))
    t = jnp.arange(w)
    lo = jnp.maximum(t - k // 2, 0)
    hi = jnp.minimum(t - k // 2 + k, w)
    tot = jnp.take(s, hi, axis=2) - jnp.take(s, lo, axis=2)
    return (tot / (hi - lo).astype(F32)[:, None]).astype(x4.dtype)


def pool_mixer(hc, rows, w_grp, scale):
    b, l, _ = hc.shape
    x4 = hc.reshape(b, rows, l // rows, D_C)
    pooled = jnp.concatenate(
        [window_mean(x4[..., g * POOL_CH:(g + 1) * POOL_CH], k) for g, k in enumerate(POOL_WINDOWS)],
        axis=-1) - x4
    pooled = pooled.reshape(b, l, len(POOL_WINDOWS), POOL_CH)
    out = jnp.einsum('blgc,gcd->blgd', pooled, w_grp).reshape(b, l, D_C)
    return out * scale


def setup_inputs(seed: int = 0) -> dict:
    key = jax.random.key(seed)
    ks = jax.random.split(key, 32)
    nrm = lambda k, s, sc: jax.random.normal(k, s, F32) * sc
    gain = lambda k, s: 1.0 + 0.01 * jax.random.normal(k, s, F32)
    x_prompt = nrm(ks[0], (BATCH, SEQ, D_MODEL), 1.0)
    x_sample = nrm(ks[1], (DEC_BATCH, DEC_SEQ, D_MODEL), 1.0)
    state_ssd = nrm(ks[2], (DEC_BATCH, N_EVEN, 2, SSD_HEADS, SSD_HEAD_DIM, D_STATE), 0.5)
    c = nrm(ks[3], (DEC_BATCH, D_MODEL), 1.0)
    c_ctx = nrm(ks[4], (D_MODEL,), 1.0)
    w_mod = nrm(ks[5], (DEPTH, D_MODEL, N_MOD * D_MODEL), D_MODEL ** -0.5)
    b_mod = nrm(ks[6], (DEPTH, N_MOD * D_MODEL), 0.01)
    norm_mix = gain(ks[7], (DEPTH, D_MODEL))
    norm_ffn = gain(ks[8], (DEPTH, D_MODEL))
    w_in_even = nrm(ks[9], (N_EVEN, D_MODEL, IN_EVEN), D_MODEL ** -0.5)
    w_in_even = w_in_even.at[..., IN_EVEN - 2 * SSD_HEADS:].multiply(0.1)
    w_out_even = nrm(ks[10], (N_EVEN, D_A + D_B, D_MODEL), (D_A + D_B) ** -0.5)
    gmlp_norm = gain(ks[11], (N_EVEN, D_A))
    gmlp_ws = nrm(ks[12], (N_EVEN, A_GROUPS, CHUNK, CHUNK), CHUNK ** -0.5)
    gmlp_bs = nrm(ks[13], (N_EVEN, A_GROUPS, CHUNK), 0.01)
    ssd_conv_w = nrm(ks[14], (N_EVEN, D_CONV, C_CONV), D_CONV ** -0.5)
    ssd_conv_b = nrm(ks[15], (N_EVEN, C_CONV), 0.01)
    dt0 = jnp.exp(jax.random.uniform(ks[16], (N_EVEN, 2, SSD_HEADS), F32,
                                     float(np.log(1e-3)), float(np.log(1e-1))))
    ssd_dt_bias = dt0 + jnp.log(-jnp.expm1(-dt0))
    ssd_a_log = jnp.log(jax.random.uniform(ks[17], (N_EVEN, 2, SSD_HEADS), F32, 1.0, 16.0))
    ssd_d = gain(ks[18], (N_EVEN, SSD_HEADS))
    ssd_norm = gain(ks[19], (N_EVEN, D_B))
    w_in_odd = nrm(ks[20], (N_ODD, D_MODEL, D_C), D_MODEL ** -0.5)
    pool_w = nrm(ks[21], (N_ODD, len(POOL_WINDOWS), POOL_CH, POOL_CH), POOL_CH ** -0.5)
    pool_scale = gain(ks[22], (N_ODD, D_C))
    w_out_odd = nrm(ks[23], (N_ODD, D_C, D_MODEL), D_C ** -0.5)
    ffn_w1 = nrm(ks[24], (DEPTH, D_MODEL, D_FF), D_MODEL ** -0.5)
    ffn_w3 = nrm(ks[25], (DEPTH, D_MODEL, D_FF), D_MODEL ** -0.5)
    ffn_w2 = nrm(ks[26], (DEPTH, D_FF, D_MODEL), D_FF ** -0.5)
    final_norm = gain(ks[27], (D_MODEL,))
    return {"x_prompt": x_prompt, "x_sample": x_sample, "state_ssd": state_ssd, "c": c, "c_ctx": c_ctx,
            "w_mod": w_mod, "b_mod": b_mod, "norm_mix": norm_mix, "norm_ffn": norm_ffn,
            "w_in_even": w_in_even, "w_out_even": w_out_even, "gmlp_norm": gmlp_norm,
            "gmlp_ws": gmlp_ws, "gmlp_bs": gmlp_bs, "ssd_conv_w": ssd_conv_w, "ssd_conv_b": ssd_conv_b,
            "ssd_dt_bias": ssd_dt_bias, "ssd_a_log": ssd_a_log, "ssd_d": ssd_d, "ssd_norm": ssd_norm,
            "w_in_odd": w_in_odd, "pool_w": pool_w, "pool_scale": pool_scale, "w_out_odd": w_out_odd,
            "ffn_w1": ffn_w1, "ffn_w3": ffn_w3, "ffn_w2": ffn_w2, "final_norm": final_norm}


def reference(x_prompt, x_sample, state_ssd, c, c_ctx, w_mod, b_mod, norm_mix, norm_ffn,
              w_in_even, w_out_even, gmlp_norm, gmlp_ws, gmlp_bs, ssd_conv_w, ssd_conv_b,
              ssd_dt_bias, ssd_a_log, ssd_d, ssd_norm, w_in_odd, pool_w, pool_scale, w_out_odd,
              ffn_w1, ffn_w3, ffn_w2, final_norm):
    splits = [D_A, 2 * D_A, 2 * D_A + D_B, 2 * D_A + D_B + C_CONV, 2 * D_A + D_B + C_CONV + SSD_HEADS]

    def run(x, cond, rows, state0):
        finals = []
        for i in range(DEPTH):
            mod = (jax.nn.silu(cond) @ w_mod[i] + b_mod[i])[:, None, :]
            sh1, sc1, g1, sh2, sc2, g2 = jnp.split(mod, N_MOD, axis=-1)
            h = modulate(rmsnorm(x, norm_mix[i]), sh1, sc1)
            if i % 2 == 0:
                e = i // 2
                u, v, z, xbc, dt_f, dt_b = jnp.split(h @ w_in_even[e], splits, axis=-1)
                a_out = chunk_gmlp(u, v, gmlp_ws[e], gmlp_bs[e], gmlp_norm[e])
                b_out, fin = ssd_mixer(z, xbc, dt_f, dt_b, ssd_conv_w[e], ssd_conv_b[e], ssd_dt_bias[e],
                                       ssd_a_log[e], ssd_d[e], ssd_norm[e], state0[:, e])
                finals.append(fin)
                out = jnp.concatenate([a_out, b_out], axis=-1) @ w_out_even[e]
            else:
                o = i // 2
                out = pool_mixer(h @ w_in_odd[o], rows, pool_w[o], pool_scale[o]) @ w_out_odd[o]
            x = x + g1 * out
            h = modulate(rmsnorm(x, norm_ffn[i]), sh2, sc2)
            x = x + g2 * ((jax.nn.silu(h @ ffn_w1[i]) * (h @ ffn_w3[i])) @ ffn_w2[i])
        return rmsnorm(x, final_norm), jnp.stack(finals, axis=1)

    zeros = jnp.zeros((x_prompt.shape[0], N_EVEN, 2, SSD_HEADS, SSD_HEAD_DIM, D_STATE), F32)
    y_prompt, state_ssd_new = run(x_prompt, c_ctx[None, :], 1, zeros)
    rows = x_sample.shape[1] // GRID_W
    y_sample, _ = run(x_sample, c, rows, state_ssd)
    return (y_prompt, y_sample, state_ssd_new)
```

```python
import functools

import jax
import jax.numpy as jnp
from jax import lax
from jax.experimental import pallas as pl
from jax.experimental.pallas import tpu as pltpu

F32 = jnp.float32
BF16 = jnp.bfloat16
I32 = jnp.int32

EPS = 1e-6
N_MOD = 6
GRID_W = 64
CHUNK = 128
SSD_CHUNK = 128
SSD_GROUPS = 4
D_CONV = 5
POOL_WINDOWS = (2, 4, 8, 16)
LANES = 128
HALO = 16
VMEM_LIMIT_BYTES = 56 * 1024 * 1024


def _params(*sem):
    return pltpu.CompilerParams(dimension_semantics=sem, vmem_limit_bytes=VMEM_LIMIT_BYTES)


def _tile(n, pref):
    t = min(n, pref)
    assert n % t == 0, (n, pref)
    return t


def _silu(x):
    return x * jax.nn.sigmoid(x)


def _softplus(x):
    return jnp.maximum(x, 0.0) + jnp.log1p(jnp.exp(-jnp.abs(x)))


def _split3(x):
    x1 = x.astype(BF16)
    r = x - x1.astype(F32)
    x2 = r.astype(BF16)
    x3 = (r - x2.astype(F32)).astype(BF16)
    return x1, x2, x3


def _norm_mod(x, nw, shift, scale):
    ms = jnp.mean(x * x, axis=-1, keepdims=True)
    y = x * lax.rsqrt(ms + EPS) * nw
    return y * (1.0 + scale) + shift


def _dot(a, b):
    return jnp.dot(a, b, preferred_element_type=F32)


def _mod_kernel(c_ref, w_ref, b_ref, o_ref):
    s = _silu(c_ref[...])
    w = w_ref[0]
    s1 = s.astype(BF16)
    s2 = (s - s1.astype(F32)).astype(BF16)
    w1 = w.astype(BF16)
    w2 = (w - w1.astype(F32)).astype(BF16)
    o_ref[0] = (_dot(s1, w1) + _dot(s2, w1) + _dot(s1, w2)) + b_ref[0]


def _mod_call(cond, w_mod, b_mod):
    depth, d, n = w_mod.shape
    rows = cond.shape[0]
    tn = _tile(n, 1024)
    return pl.pallas_call(
        _mod_kernel,
        grid=(depth, n // tn),
        in_specs=[
            pl.BlockSpec((rows, d), lambda l, j: (0, 0)),
            pl.BlockSpec((1, d, tn), lambda l, j: (l, 0, j)),
            pl.BlockSpec((1, 1, tn), lambda l, j: (l, 0, j)),
        ],
        out_specs=pl.BlockSpec((1, rows, tn), lambda l, j: (l, 0, j)),
        out_shape=jax.ShapeDtypeStruct((depth, rows, n), F32),
        compiler_params=_params("parallel", "parallel"),
        name="mod",
    )(cond, w_mod, b_mod.reshape(depth, 1, n))


def _in_kernel(x_ref, mod_ref, nw_ref, w_ref, *rest, with_dt):
    if with_dt:
        wdt_ref, wdtT_ref, o_ref, dt_ref, dtT_ref, h_ref = rest
    else:
        o_ref, h_ref = rest

    @pl.when(pl.program_id(1) == 0)
    def _():
        m = mod_ref[0]
        h = _norm_mod(x_ref[...], nw_ref[...], m[0:1], m[1:2]).astype(BF16)
        h_ref[...] = h
        if with_dt:
            dt_ref[...] = _dot(h, wdt_ref[...])
            dtT_ref[...] = lax.dot_general(wdtT_ref[...], h, (((1,), (1,)), ((), ())),
                                           preferred_element_type=F32)

    o_ref[...] = _dot(h_ref[...], w_ref[...]).astype(o_ref.dtype)


def _in_call(x, mod, nw, w, seq_len, wdt=None, wdtT=None):
    t, d = x.shape
    n = w.shape[1]
    nb = mod.shape[0]
    tm = _tile(t, 1024 if nb == 1 else min(1024, seq_len))
    tn = _tile(n, 1024)
    cid = (lambda i: 0) if nb == 1 else (lambda i: (i * tm) // seq_len)
    with_dt = wdt is not None
    in_specs = [
        pl.BlockSpec((tm, d), lambda i, j: (i, 0)),
        pl.BlockSpec((1, N_MOD, d), lambda i, j: (cid(i), 0, 0)),
        pl.BlockSpec((1, d), lambda i, j: (0, 0)),
        pl.BlockSpec((d, tn), lambda i, j: (0, j)),
    ]
    out_specs = [pl.BlockSpec((tm, tn), lambda i, j: (i, j))]
    out_shape = [jax.ShapeDtypeStruct((t, n), BF16)]
    args = [x, mod, nw, w]
    if with_dt:
        h2 = wdtT.shape[0]
        in_specs += [pl.BlockSpec((d, LANES), lambda i, j: (0, 0)),
                     pl.BlockSpec((h2, d), lambda i, j: (0, 0))]
        out_specs += [pl.BlockSpec((tm, LANES), lambda i, j: (i, 0)),
                      pl.BlockSpec((h2, tm), lambda i, j: (0, i))]
        out_shape += [jax.ShapeDtypeStruct((t, LANES), F32), jax.ShapeDtypeStruct((h2, t), F32)]
        args += [wdt, wdtT]
    return pl.pallas_call(
        functools.partial(_in_kernel, with_dt=with_dt),
        grid=(t // tm, n // tn),
        in_specs=in_specs,
        out_specs=out_specs,
        out_shape=out_shape,
        scratch_shapes=[pltpu.VMEM((tm, d), BF16)],
        compiler_params=_params("parallel", "arbitrary"),
        name="in_proj",
    )(*args)


def _gmlp_kernel(u_ref, v_ref, ws_ref, bsT_ref, vn_ref, o_ref, *, groups, ch):
    for g in range(groups):
        sl = slice(g * ch, (g + 1) * ch)
        v = jax.nn.gelu(v_ref[:, sl].astype(F32))
        ms = jnp.mean(v * v, axis=-1, keepdims=True)
        vn = v * lax.rsqrt(ms + EPS) * vn_ref[:, sl]
        sv = _dot(ws_ref[g], vn.astype(BF16)) + bsT_ref[:, g:g + 1]
        u = jax.nn.gelu(u_ref[:, sl].astype(F32))
        o_ref[:, sl] = (u * sv).astype(o_ref.dtype)


def _gmlp_call(proj, ws, bsT, vnorm, d_a):
    t = proj.shape[0]
    groups = ws.shape[0]
    return pl.pallas_call(
        functools.partial(_gmlp_kernel, groups=groups, ch=d_a // groups),
        grid=(t // CHUNK,),
        in_specs=[
            pl.BlockSpec((CHUNK, d_a), lambda i: (i, 0)),
            pl.BlockSpec((CHUNK, d_a), lambda i: (i, 1)),
            pl.BlockSpec((groups, CHUNK, CHUNK), lambda i: (0, 0, 0)),
            pl.BlockSpec((CHUNK, groups), lambda i: (0, 0)),
            pl.BlockSpec((1, d_a), lambda i: (0, 0)),
        ],
        out_specs=pl.BlockSpec((CHUNK, d_a), lambda i: (i, 0)),
        out_shape=jax.ShapeDtypeStruct((t, d_a), BF16),
        compiler_params=_params("parallel"),
        name="gmlp",
    )(proj, proj, ws, bsT, vnorm)


def _conv_kernel(cur_ref, prev_ref, next_ref, w_ref, b_ref, o_ref, *, nc):
    c = pl.program_id(1)
    q = cur_ref.shape[0]
    cur = cur_ref[...].astype(F32)
    prev = jnp.where(c > 0, prev_ref[...].astype(F32), 0.0)
    nxt = jnp.where(c < nc - 1, next_ref[...].astype(F32), 0.0)
    ext = jnp.concatenate([prev, cur, nxt], axis=0)
    rows = q + 2 * HALO
    acc = b_ref[...]
    for k in range(D_CONV):
        shift = (D_CONV // 2 - k) % rows
        r = ext if shift == 0 else pltpu.roll(ext, shift, axis=0)
        acc = acc + w_ref[k:k + 1, :] * r[HALO:HALO + q]
    o_ref[...] = _silu(acc).astype(o_ref.dtype)


def _conv_call(proj, conv_w, conv_b, seq_len, col_block):
    t = proj.shape[0]
    cc = conv_w.shape[1]
    q = SSD_CHUNK
    nc = seq_len // q
    nseq = t // seq_len
    per = q // HALO
    last = t // HALO - 1
    return pl.pallas_call(
        functools.partial(_conv_kernel, nc=nc),
        grid=(nseq, nc),
        in_specs=[
            pl.BlockSpec((q, cc), lambda s, c: (s * nc + c, col_block)),
            pl.BlockSpec((HALO, cc), lambda s, c: (jnp.maximum((s * nc + c) * per - 1, 0), col_block)),
            pl.BlockSpec((HALO, cc), lambda s, c: (jnp.minimum((s * nc + c + 1) * per, last), col_block)),
            pl.BlockSpec((D_CONV, cc), lambda s, c: (0, 0)),
            pl.BlockSpec((1, cc), lambda s, c: (0, 0)),
        ],
        out_specs=pl.BlockSpec((q, cc), lambda s, c: (s * nc + c, 0)),
        out_shape=jax.ShapeDtypeStruct((t, cc), BF16),
        compiler_params=_params("parallel", "arbitrary"),
        name="conv",
    )(proj, proj, proj, conv_w, conv_b)


def _ssd_kernel(xbc_ref, dt_ref, dtT_ref, z_ref, dtb_ref, dtbT_ref, alog_ref, alogT_ref, dsk_ref, nw_ref,
                *rest, nc, heads, hdim, groups, nstate, has_h0, want_state):
    rest = list(rest)
    h0f_ref = h0b_ref = sf_ref = sb_ref = None
    if has_h0:
        h0f_ref, h0b_ref = rest[:2]
        rest = rest[2:]
    y_ref = rest.pop(0)
    if want_state:
        sf_ref, sb_ref = rest[:2]
        rest = rest[2:]
    st_ref, yf_ref = rest

    q = SSD_CHUNK
    db = heads * hdim
    gn = groups * nstate
    hpg = heads // groups
    gw = hpg * hdim
    t = pl.program_id(1)
    rev = t >= nc
    c = jnp.where(rev, 2 * nc - 1 - t, t)

    def dsel(fwd, bwd):
        return jnp.where(rev, bwd, fwd)

    @pl.when(t == 0)
    def _():
        st_ref[...] = h0f_ref[0] if has_h0 else jnp.zeros_like(st_ref)

    @pl.when(t == nc)
    def _():
        st_ref[...] = h0b_ref[0] if has_h0 else jnp.zeros_like(st_ref)

    dt = _softplus(dsel(dt_ref[:, 0:heads], dt_ref[:, heads:2 * heads])
                   + dsel(dtb_ref[0:1, :], dtb_ref[1:2, :]))
    a = dt * (-jnp.exp(dsel(alog_ref[0:1, :], alog_ref[1:2, :])))
    aT = (_softplus(dsel(dtT_ref[0:heads, :], dtT_ref[heads:2 * heads, :])
                    + dsel(dtbT_ref[0:heads, :], dtbT_ref[heads:2 * heads, :]))
          * (-jnp.exp(dsel(alogT_ref[0:heads, :], alogT_ref[heads:2 * heads, :]))))

    ri = lax.broadcasted_iota(I32, (q, q), 0)
    ci = lax.broadcasted_iota(I32, (q, q), 1)
    lowf = jnp.where(ri >= ci, 1.0, 0.0)
    uppf = jnp.where(ci >= ri, 1.0, 0.0)
    trif = dsel(lowf, uppf)
    mask = trif > 0.5
    tri = trif.astype(BF16)
    triT = dsel(uppf, lowf).astype(BF16)
    a1, a2, a3 = _split3(a)
    a_cs = (_dot(tri, a1) + _dot(tri, a2)) + _dot(tri, a3)
    b1, b2, b3 = _split3(aT)
    a_csT = (_dot(b1, triT) + _dot(b2, triT)) + _dot(b3, triT)

    e_cs = jnp.exp(a_cs)
    tot = dsel(a_cs[q - 1:q, :], a_cs[0:1, :])
    dec = jnp.exp(tot - a_cs)

    assert hdim & (hdim - 1) == 0
    eh = lax.broadcasted_iota(I32, (heads, db), 0)
    ec = lax.shift_right_logical(lax.broadcasted_iota(I32, (heads, db), 1), hdim.bit_length() - 1)
    expand = jnp.where(eh == ec, 1.0, 0.0).astype(BF16)
    stack = jnp.concatenate([dt, e_cs, dec], axis=0)
    s_hi = stack.astype(BF16)
    s_lo = (stack - s_hi.astype(F32)).astype(BF16)
    ex = _dot(s_hi, expand) + _dot(s_lo, expand)
    dt_x, e_x, dec_x = ex[0:q], ex[q:2 * q], ex[2 * q:3 * q]
    etot_row = dsel(e_x[q - 1:q, :], e_x[0:1, :])

    xs = xbc_ref[:, 0:db].astype(F32)
    xg = xs * dt_x
    xgb = xg.astype(BF16)
    xdb = (xg * dec_x).astype(BF16)
    stb = st_ref[...].astype(BF16)
    lane = lax.broadcasted_iota(I32, (q, 2 * hdim), 1)

    ys = []
    new_states = []
    for g in range(groups):
        bg = xbc_ref[:, db + g * nstate:db + (g + 1) * nstate]
        cg = xbc_ref[:, db + gn + g * nstate:db + gn + (g + 1) * nstate]
        cb = lax.dot_general(cg, bg, (((1,), (1,)), ((), ())), preferred_element_type=F32)
        yo = _dot(cg, stb[:, g * gw:(g + 1) * gw])
        for pr in range(hpg // 2):
            h0 = g * hpg + 2 * pr
            ls = []
            for h in (h0, h0 + 1):
                d = a_cs[:, h:h + 1] - a_csT[h:h + 1, :]
                ls.append(jnp.where(mask, jnp.exp(d), 0.0) * cb)
            lhs = jnp.concatenate(ls, axis=1).astype(BF16)
            xp = xgb[:, h0 * hdim:(h0 + 2) * hdim]
            zero = jnp.zeros_like(xp)
            rhs = jnp.concatenate([jnp.where(lane < hdim, xp, zero),
                                   jnp.where(lane >= hdim, xp, zero)], axis=0)
            yd = _dot(lhs, rhs)
            ys.append(yd + yo[:, 2 * pr * hdim:(2 * pr + 2) * hdim] * e_x[:, h0 * hdim:(h0 + 2) * hdim])
        new_states.append(lax.dot_general(bg, xdb[:, g * gw:(g + 1) * gw], (((0,), (0,)), ((), ())),
                                          preferred_element_type=F32))
    y = jnp.concatenate(ys, axis=1)
    st_ref[...] = st_ref[...] * etot_row + jnp.concatenate(new_states, axis=1)

    row0 = pl.multiple_of(c * q, q)

    @pl.when(jnp.logical_not(rev))
    def _():
        yf_ref[pl.ds(row0, q), :] = y.astype(yf_ref.dtype)

    @pl.when(rev)
    def _():
        yt = (yf_ref[pl.ds(row0, q), :].astype(F32) + y) + dsk_ref[...] * xs
        gz = yt * _silu(z_ref[...].astype(F32))
        ms = jnp.mean(gz * gz, axis=-1, keepdims=True)
        y_ref[...] = (gz * lax.rsqrt(ms + EPS) * nw_ref[...]).astype(y_ref.dtype)

    if want_state:
        @pl.when(t == nc - 1)
        def _():
            sf_ref[0] = st_ref[...].T

        @pl.when(t == 2 * nc - 1)
        def _():
            sb_ref[0] = st_ref[...].T


def _ssd_call(xbc, dt, dtT, proj, z_block, dt_bias, a_log, d_skip_x, norm_w, seq_len, heads, hdim, nstate,
              h0T=None, want_state=False):
    t, cc = xbc.shape
    q = SSD_CHUNK
    nc = seq_len // q
    nseq = t // seq_len
    db = heads * hdim
    h2 = 2 * heads
    has_h0 = h0T is not None

    def chunk(tt):
        return jnp.where(tt >= nc, 2 * nc - 1 - tt, tt)

    def late(tt):
        return jnp.where(tt >= nc, 2 * nc - 1 - tt, nc - 1)

    in_specs = [
        pl.BlockSpec((q, cc), lambda s, tt: (s * nc + chunk(tt), 0)),
        pl.BlockSpec((q, LANES), lambda s, tt: (s * nc + chunk(tt), 0)),
        pl.BlockSpec((h2, q), lambda s, tt: (0, s * nc + chunk(tt))),
        pl.BlockSpec((q, db), lambda s, tt: (s * nc + late(tt), z_block)),
        pl.BlockSpec((2, heads), lambda s, tt: (0, 0)),
        pl.BlockSpec((h2, 1), lambda s, tt: (0, 0)),
        pl.BlockSpec((2, heads), lambda s, tt: (0, 0)),
        pl.BlockSpec((h2, 1), lambda s, tt: (0, 0)),
        pl.BlockSpec((1, db), lambda s, tt: (0, 0)),
        pl.BlockSpec((1, db), lambda s, tt: (0, 0)),
    ]
    args = [xbc, dt, dtT, proj, dt_bias, dt_bias.reshape(h2, 1), a_log, a_log.reshape(h2, 1), d_skip_x, norm_w]
    if has_h0:
        in_specs += [pl.BlockSpec((1, nstate, db), lambda s, tt: (s, 0, 0)),
                     pl.BlockSpec((1, nstate, db), lambda s, tt: (s, 0, 0))]
        args += [h0T[:, 0], h0T[:, 1]]
    out_specs = [pl.BlockSpec((q, db), lambda s, tt: (s * nc + late(tt), 0))]
    out_shape = [jax.ShapeDtypeStruct((t, db), BF16)]
    if want_state:
        out_specs += [pl.BlockSpec((1, db, nstate), lambda s, tt: (s, 0, 0))] * 2
        out_shape += [jax.ShapeDtypeStruct((nseq, db, nstate), F32)] * 2
    return pl.pallas_call(
        functools.partial(_ssd_kernel, nc=nc, heads=heads, hdim=hdim, groups=SSD_GROUPS, nstate=nstate,
                          has_h0=has_h0, want_state=want_state),
        grid=(nseq, 2 * nc),
        in_specs=in_specs,
        out_specs=out_specs,
        out_shape=out_shape,
        scratch_shapes=[pltpu.VMEM((nstate, db), F32), pltpu.VMEM((seq_len, db), BF16)],
        compiler_params=_params("parallel", "arbitrary"),
        name="ssd",
    )(*args)


def _out_even_kernel(a_ref, b_ref, wa_ref, wb_ref, x_ref, mod_ref, o_ref):
    acc = _dot(a_ref[...], wa_ref[...]) + _dot(b_ref[...], wb_ref[...])
    o_ref[...] = x_ref[...] + mod_ref[0][2:3] * acc


def _out_even_call(a, b, w, x, mod, seq_len):
    t, d = x.shape
    d_a, d_b = a.shape[1], b.shape[1]
    assert d_a == d_b
    nb = mod.shape[0]
    tm = _tile(t, 512 if nb == 1 else min(512, seq_len))
    tn = _tile(d, 1024)
    cid = (lambda i: 0) if nb == 1 else (lambda i: (i * tm) // seq_len)
    return pl.pallas_call(
        _out_even_kernel,
        grid=(t // tm, d // tn),
        in_specs=[
            pl.BlockSpec((tm, d_a), lambda i, j: (i, 0)),
            pl.BlockSpec((tm, d_b), lambda i, j: (i, 0)),
            pl.BlockSpec((d_a, tn), lambda i, j: (0, j)),
            pl.BlockSpec((d_b, tn), lambda i, j: (1, j)),
            pl.BlockSpec((tm, tn), lambda i, j: (i, j)),
            pl.BlockSpec((1, N_MOD, tn), lambda i, j: (cid(i), 0, j)),
        ],
        out_specs=pl.BlockSpec((tm, tn), lambda i, j: (i, j)),
        out_shape=jax.ShapeDtypeStruct((t, d), F32),
        compiler_params=_params("parallel", "parallel"),
        name="out_even",
    )(a, b, w, w, x, mod)


def _pool_kernel(hc_ref, pw_ref, ps_ref, wo_ref, x_ref, mod_ref, o_ref, *, win, sub):
    tm, d_c = hc_ref.shape
    ng = len(POOL_WINDOWS)
    pch = d_c // ng
    assert win & (win - 1) == 0
    ri = lax.broadcasted_iota(I32, (sub, sub), 0)
    ci = lax.broadcasted_iota(I32, (sub, sub), 1)
    same = lax.shift_right_logical(ri, win.bit_length() - 1) == lax.shift_right_logical(ci, win.bit_length() - 1)
    pos = lax.broadcasted_iota(I32, (sub, 1), 0) & (win - 1)
    mixed = []
    for g, k in enumerate(POOL_WINDOWS):
        off = ci - ri
        band = jnp.where(same & (off >= -(k // 2)) & (off < k - k // 2), 1.0, 0.0).astype(BF16)
        lo = jnp.maximum(pos - k // 2, 0)
        hi = jnp.minimum(pos - k // 2 + k, win)
        cnt = (hi - lo).astype(F32)
        pooled = []
        for r in range(tm // sub):
            xb = hc_ref[r * sub:(r + 1) * sub, g * pch:(g + 1) * pch]
            pooled.append(_dot(band, xb) / cnt - xb.astype(F32))
        pg = jnp.concatenate(pooled, axis=0).astype(BF16)
        mixed.append(_dot(pg, pw_ref[g]) * ps_ref[:, g * pch:(g + 1) * pch])
    mix = jnp.concatenate(mixed, axis=1).astype(BF16)
    o_ref[...] = x_ref[...] + mod_ref[0][2:3] * _dot(mix, wo_ref[...])


def _pool_call(hc, pool_w, pool_scale, w_out, x, mod, seq_len, win):
    t, d = x.shape
    d_c = hc.shape[1]
    ng, pch, _ = pool_w.shape
    nb = mod.shape[0]
    sub = max(win, 128)
    assert sub % win == 0 and seq_len % sub == 0
    tm = _tile(t, 512 if nb == 1 else min(512, seq_len))
    assert tm % sub == 0
    cid = (lambda i: 0) if nb == 1 else (lambda i: (i * tm) // seq_len)
    return pl.pallas_call(
        functools.partial(_pool_kernel, win=win, sub=sub),
        grid=(t // tm,),
        in_specs=[
            pl.BlockSpec((tm, d_c), lambda i: (i, 0)),
            pl.BlockSpec((ng, pch, pch), lambda i: (0, 0, 0)),
            pl.BlockSpec((1, d_c), lambda i: (0, 0)),
            pl.BlockSpec((d_c, d), lambda i: (0, 0)),
            pl.BlockSpec((tm, d), lambda i: (i, 0)),
            pl.BlockSpec((1, N_MOD, d), lambda i: (cid(i), 0, 0)),
        ],
        out_specs=pl.BlockSpec((tm, d), lambda i: (i, 0)),
        out_shape=jax.ShapeDtypeStruct((t, d), F32),
        compiler_params=_params("parallel"),
        name="pool_out",
    )(hc, pool_w, pool_scale, w_out, x, mod)


def _ffn_kernel(x_ref, mod_ref, nw_ref, w1_ref, w3_ref, w2_ref, fn_ref, o_ref, h_ref, acc_ref, *, nf, final):
    f = pl.program_id(1)

    @pl.when(f == 0)
    def _():
        m = mod_ref[0]
        h_ref[...] = _norm_mod(x_ref[...], nw_ref[...], m[3:4], m[4:5]).astype(BF16)

    h = h_ref[...]
    gate = _dot(h, w1_ref[...])
    up = _dot(h, w3_ref[...])
    p = _dot((_silu(gate) * up).astype(BF16), w2_ref[...])

    @pl.when(f == 0)
    def _():
        acc_ref[...] = p

    @pl.when((f > 0) & (f < nf - 1))
    def _():
        acc_ref[...] += p

    @pl.when(f == nf - 1)
    def _():
        y = x_ref[...] + mod_ref[0][5:6] * (acc_ref[...] + p)
        if final:
            ms = jnp.mean(y * y, axis=-1, keepdims=True)
            y = y * lax.rsqrt(ms + EPS) * fn_ref[...]
        o_ref[...] = y


def _ffn_call(x, mod, nw, w1, w3, w2, final_nw, seq_len, final):
    t, d = x.shape
    dff = w1.shape[1]
    nb = mod.shape[0]
    tm = _tile(t, 512 if nb == 1 else min(512, seq_len))
    tf = _tile(dff, 512)
    nf = dff // tf
    assert nf >= 2
    cid = (lambda i: 0) if nb == 1 else (lambda i: (i * tm) // seq_len)
    return pl.pallas_call(
        functools.partial(_ffn_kernel, nf=nf, final=final),
        grid=(t // tm, nf),
        in_specs=[
            pl.BlockSpec((tm, d), lambda i, f: (i, 0)),
            pl.BlockSpec((1, N_MOD, d), lambda i, f: (cid(i), 0, 0)),
            pl.BlockSpec((1, d), lambda i, f: (0, 0)),
            pl.BlockSpec((d, tf), lambda i, f: (0, f)),
            pl.BlockSpec((d, tf), lambda i, f: (0, f)),
            pl.BlockSpec((tf, d), lambda i, f: (f, 0)),
            pl.BlockSpec((1, d), lambda i, f: (0, 0)),
        ],
        out_specs=pl.BlockSpec((tm, d), lambda i, f: (i, 0)),
        out_shape=jax.ShapeDtypeStruct((t, d), F32),
        scratch_shapes=[pltpu.VMEM((tm, d), BF16), pltpu.VMEM((tm, d), F32)],
        compiler_params=_params("parallel", "arbitrary"),
        name="ffn",
    )(x, mod, nw, w1, w3, w2, final_nw)


def _run_group(x, seq_len, mods, win, h0T, want_state, p):
    depth = mods.shape[0]
    finals = []
    for i in range(depth):
        mod = mods[i]
        if i % 2 == 0:
            e = i // 2
            proj, dt, dtT = _in_call(x, mod, p["norm_mix"][i], p["w_in_main"][e], seq_len,
                                     p["w_in_dt"][e], p["w_in_dtT"][e])
            a_out = _gmlp_call(proj, p["gmlp_ws"][e], p["gmlp_bsT"][e], p["gmlp_norm"][e], p["d_a"])
            xbc = _conv_call(proj, p["ssd_conv_w"][e], p["ssd_conv_b"][e], seq_len, p["xbc_block"])
            res = _ssd_call(xbc, dt, dtT, proj, p["z_block"], p["ssd_dt_bias"][e], p["ssd_a_log"][e],
                            p["ssd_d_x"][e], p["ssd_norm"][e], seq_len, p["heads"], p["hdim"], p["nstate"],
                            None if h0T is None else h0T[:, e], want_state)
            if want_state:
                finals.append(jnp.stack([res[1], res[2]], axis=1))
            x = _out_even_call(a_out, res[0], p["w_out_even"][e], x, mod, seq_len)
        else:
            o = i // 2
            (hc,) = _in_call(x, mod, p["norm_mix"][i], p["w_in_odd"][o], seq_len)
            x = _pool_call(hc, p["pool_w"][o], p["pool_scale"][o], p["w_out_odd"][o], x, mod, seq_len, win)
        x = _ffn_call(x, mod, p["norm_ffn"][i], p["ffn_w1"][i], p["ffn_w3"][i], p["ffn_w2"][i],
                      p["final_norm"], seq_len, final=(i == depth - 1))
    return x, finals


def kernel(x_prompt, x_sample, state_ssd, c, c_ctx, w_mod, b_mod, norm_mix, norm_ffn, w_in_even, w_out_even, gmlp_norm, gmlp_ws, gmlp_bs, ssd_conv_w, ssd_conv_b, ssd_dt_bias, ssd_a_log, ssd_d, ssd_norm, w_in_odd, pool_w, pool_scale, w_out_odd, ffn_w1, ffn_w3, ffn_w2, final_norm):
    batch, seq, d = x_prompt.shape
    dec_batch, dec_seq, _ = x_sample.shape
    depth = w_mod.shape[0]
    n_even = w_in_even.shape[0]
    heads, hdim, nstate = state_ssd.shape[3:]
    d_b = heads * hdim
    d_a = gmlp_norm.shape[1]
    cc = ssd_conv_w.shape[2]
    n_main = 2 * d_a + d_b + cc
    h2 = 2 * heads
    assert w_in_even.shape[2] == n_main + h2 and h2 <= LANES
    assert (2 * d_a) % d_b == 0 and n_main - cc == (n_main - cc) // cc * cc

    n_cond = 1 + dec_batch
    rows = -(-n_cond // 8) * 8
    cond = jnp.concatenate([c_ctx[None], c, jnp.zeros((rows - n_cond, d), F32)], axis=0)
    mods = _mod_call(cond, w_mod, b_mod).reshape(depth, rows, N_MOD, d)

    w_dt = w_in_even[:, :, n_main:]
    p = dict(
        d_a=d_a, heads=heads, hdim=hdim, nstate=nstate,
        z_block=(2 * d_a) // d_b, xbc_block=(n_main - cc) // cc,
        norm_mix=norm_mix.reshape(depth, 1, d), norm_ffn=norm_ffn.reshape(depth, 1, d),
        w_in_main=w_in_even[:, :, :n_main].astype(BF16),
        w_in_dt=jnp.pad(w_dt, ((0, 0), (0, 0), (0, LANES - h2))).astype(BF16),
        w_in_dtT=jnp.swapaxes(w_dt, 1, 2).astype(BF16),
        w_out_even=w_out_even.astype(BF16),
        gmlp_norm=gmlp_norm.reshape(n_even, 1, d_a), gmlp_ws=gmlp_ws.astype(BF16),
        gmlp_bsT=jnp.swapaxes(gmlp_bs, 1, 2),
        ssd_conv_w=ssd_conv_w, ssd_conv_b=ssd_conv_b.reshape(n_even, 1, cc),
        ssd_dt_bias=ssd_dt_bias, ssd_a_log=ssd_a_log,
        ssd_d_x=jnp.repeat(ssd_d, hdim, axis=1).reshape(n_even, 1, d_b),
        ssd_norm=ssd_norm.reshape(n_even, 1, d_b),
        w_in_odd=w_in_odd.astype(BF16), pool_w=pool_w.astype(BF16),
        pool_scale=pool_scale.reshape(-1, 1, d), w_out_odd=w_out_odd.astype(BF16),
        ffn_w1=ffn_w1.astype(BF16), ffn_w3=ffn_w3.astype(BF16), ffn_w2=ffn_w2.astype(BF16),
        final_norm=final_norm.reshape(1, d),
    )

    y_p, finals = _run_group(x_prompt.reshape(batch * seq, d), seq, mods[:, 0:1], seq, None, True, p)
    h0T = jnp.swapaxes(state_ssd.reshape(dec_batch, n_even, 2, d_b, nstate), 3, 4)
    y_s, _ = _run_group(x_sample.reshape(dec_batch * dec_seq, d), dec_seq, mods[:, 1:n_cond], GRID_W, h0T, False, p)

    state_new = jnp.stack(finals, axis=1).reshape(batch, n_even, 2, heads, hdim, nstate)
    return (y_p.reshape(batch, seq, d), y_s.reshape(dec_batch, dec_seq, d), state_new)
```

```python
import functools

import jax
import jax.numpy as jnp
from jax import lax
from jax.experimental import pallas as pl
from jax.experimental.pallas import tpu as pltpu

F32 = jnp.float32
BF16 = jnp.bfloat16
I32 = jnp.int32

EPS = 1e-6
N_MOD = 6
GRID_W = 64
CHUNK = 128
SSD_CHUNK = 128
SSD_GROUPS = 4
D_CONV = 5
POOL_WINDOWS = (2, 4, 8, 16)
LANES = 128
HALO = 16
NORM_ROWS = 32
NORM_UNROLL = 4
CONV_COLS = 512
LOG2E = 1.4426950408889634
VMEM_LIMIT_BYTES = 56 * 1024 * 1024


def _params(*sem):
    return pltpu.CompilerParams(dimension_semantics=sem, vmem_limit_bytes=VMEM_LIMIT_BYTES)


def _tile(n, pref):
    t = min(n, pref)
    assert n % t == 0, (n, pref)
    return t


def _row_tile(t, seq_len, n_cond, pref):
    if n_cond == 1:
        return _tile(t, pref), (lambda i: 0)
    tm = _tile(seq_len, min(pref, seq_len))
    return tm, (lambda i: (i * tm) // seq_len)


def _silu(x):
    return x * jax.nn.sigmoid(x)


def _softplus(x):
    return jnp.maximum(x, 0.0) + jnp.log1p(jnp.exp(-jnp.abs(x)))


def _split3(x):
    x1 = x.astype(BF16)
    r = x - x1.astype(F32)
    x2 = r.astype(BF16)
    x3 = (r - x2.astype(F32)).astype(BF16)
    return x1, x2, x3


def _dot(a, b):
    return jnp.dot(a, b, preferred_element_type=F32)


def _norm_mod_rows(x_ref, nw_ref, shift, scale, h_ref, copy_ref=None):
    gain = nw_ref[...] * (1.0 + scale)
    rows = min(NORM_ROWS, x_ref.shape[0])

    def body(r, carry):
        sl = pl.ds(pl.multiple_of(r * rows, rows), rows)
        x = x_ref[sl, :]
        ms = jnp.mean(x * x, axis=-1, keepdims=True)
        h_ref[sl, :] = (x * lax.rsqrt(ms + EPS) * gain + shift).astype(h_ref.dtype)
        if copy_ref is not None:
            copy_ref[sl, :] = x
        return carry

    lax.fori_loop(0, x_ref.shape[0] // rows, body, 0, unroll=NORM_UNROLL)


def _mod_kernel(c_ref, w_ref, b_ref, o_ref):
    s = _silu(c_ref[...])
    w = w_ref[...]
    s1 = s.astype(BF16)
    s2 = (s - s1.astype(F32)).astype(BF16)
    w1 = w.astype(BF16)
    w2 = (w - w1.astype(F32)).astype(BF16)
    o_ref[...] = (_dot(s1, w1) + _dot(s2, w1) + _dot(s1, w2)) + b_ref[...]


def _mod_call(cond, w_mod, b_mod):
    depth, d, n = w_mod.shape
    rows = cond.shape[0]
    tn = _tile(n, 1024)
    return pl.pallas_call(
        _mod_kernel,
        grid=(depth, n // tn),
        in_specs=[
            pl.BlockSpec((rows, d), lambda l, j: (0, 0)),
            pl.BlockSpec((None, d, tn), lambda l, j: (l, 0, j)),
            pl.BlockSpec((None, 1, tn), lambda l, j: (l, 0, j)),
        ],
        out_specs=pl.BlockSpec((None, rows, tn), lambda l, j: (l, 0, j)),
        out_shape=jax.ShapeDtypeStruct((depth, rows, n), F32),
        compiler_params=_params("parallel", "parallel"),
        name="mod",
    )(cond, w_mod, b_mod.reshape(depth, 1, n))


def _in_kernel(x_ref, mod_ref, nw_ref, w_ref, *rest, with_dt):
    if with_dt:
        wdtT_ref, o_ref, dtT_ref, h_ref = rest
    else:
        o_ref, h_ref = rest

    @pl.when(pl.program_id(1) == 0)
    def _():
        _norm_mod_rows(x_ref, nw_ref, mod_ref[0:1, :], mod_ref[1:2, :], h_ref)
        if with_dt:
            dtT_ref[...] = lax.dot_general(wdtT_ref[...], h_ref[...], (((1,), (1,)), ((), ())),
                                           preferred_element_type=F32)

    o_ref[...] = _dot(h_ref[...], w_ref[...]).astype(o_ref.dtype)


def _in_call(x, mods, norm_w, w, layer, wl, n, seq_len, tn_pref, wdtT=None):
    t, d = x.shape
    tm, cid = _row_tile(t, seq_len, mods.shape[1], 1024)
    tn = _tile(n, tn_pref)
    with_dt = wdtT is not None
    in_specs = [
        pl.BlockSpec((tm, d), lambda i, j: (i, 0)),
        pl.BlockSpec((None, None, N_MOD, d), lambda i, j: (layer, cid(i), 0, 0)),
        pl.BlockSpec((None, 1, d), lambda i, j: (layer, 0, 0)),
        pl.BlockSpec((None, d, tn), lambda i, j: (wl, 0, j)),
    ]
    out_specs = [pl.BlockSpec((tm, tn), lambda i, j: (i, j))]
    out_shape = [jax.ShapeDtypeStruct((t, n), BF16)]
    args = [x, mods, norm_w, w]
    if with_dt:
        h2 = wdtT.shape[1]
        in_specs.append(pl.BlockSpec((None, h2, d), lambda i, j: (wl, 0, 0)))
        out_specs.append(pl.BlockSpec((h2, tm), lambda i, j: (0, i)))
        out_shape.append(jax.ShapeDtypeStruct((h2, t), F32))
        args.append(wdtT)
    return pl.pallas_call(
        functools.partial(_in_kernel, with_dt=with_dt),
        grid=(t // tm, n // tn),
        in_specs=in_specs,
        out_specs=out_specs,
        out_shape=out_shape,
        scratch_shapes=[pltpu.VMEM((tm, d), BF16)],
        compiler_params=_params("parallel", "arbitrary"),
        name="in_proj",
    )(*args)


def _gmlp_kernel(u_ref, v_ref, ws_ref, bsT_ref, vn_ref, o_ref, *, groups, ch):
    def body(c, carry):
        rows = pl.ds(pl.multiple_of(c * CHUNK, CHUNK), CHUNK)
        for g in range(groups):
            sl = slice(g * ch, (g + 1) * ch)
            v = jax.nn.gelu(v_ref[rows, sl].astype(F32))
            ms = jnp.mean(v * v, axis=-1, keepdims=True)
            vn = v * lax.rsqrt(ms + EPS) * vn_ref[:, sl]
            sv = _dot(ws_ref[g], vn.astype(BF16)) + bsT_ref[:, g:g + 1]
            u = jax.nn.gelu(u_ref[rows, sl].astype(F32))
            o_ref[rows, sl] = (u * sv).astype(o_ref.dtype)
        return carry

    lax.fori_loop(0, u_ref.shape[0] // CHUNK, body, 0)


def _gmlp_call(proj, ws, bsT, vnorm, layer, d_a, seq_len):
    t = proj.shape[0]
    groups = ws.shape[1]
    tm = _tile(seq_len, 4 * CHUNK)
    return pl.pallas_call(
        functools.partial(_gmlp_kernel, groups=groups, ch=d_a // groups),
        grid=(t // tm,),
        in_specs=[
            pl.BlockSpec((tm, d_a), lambda i: (i, 0)),
            pl.BlockSpec((tm, d_a), lambda i: (i, 1)),
            pl.BlockSpec((None, groups, CHUNK, CHUNK), lambda i: (layer, 0, 0, 0)),
            pl.BlockSpec((None, CHUNK, groups), lambda i: (layer, 0, 0)),
            pl.BlockSpec((None, 1, d_a), lambda i: (layer, 0, 0)),
        ],
        out_specs=pl.BlockSpec((tm, d_a), lambda i: (i, 0)),
        out_shape=jax.ShapeDtypeStruct((t, d_a), BF16),
        compiler_params=_params("parallel"),
        name="gmlp",
    )(proj, proj, ws, bsT, vnorm)


def _conv_kernel(cur_ref, prev_ref, next_ref, w_ref, b_ref, o_ref, *, nc):
    c = pl.program_id(1)
    q, cc = cur_ref.shape
    rows = q + 2 * HALO
    ri = lax.broadcasted_iota(I32, (q, rows), 0)
    ci = lax.broadcasted_iota(I32, (q, rows), 1)
    picks = [None if k == D_CONV // 2 else
             jnp.where(ci == ri + (HALO + k - D_CONV // 2), 1.0, 0.0).astype(BF16) for k in range(D_CONV)]
    cw = min(CONV_COLS, cc)
    for n0 in range(0, cc, cw):
        cs = slice(n0, n0 + cw)
        cur = cur_ref[:, cs]
        prev = jnp.where(c > 0, prev_ref[:, cs], jnp.zeros((HALO, cw), BF16))
        nxt = jnp.where(c < nc - 1, next_ref[:, cs], jnp.zeros((HALO, cw), BF16))
        ext = jnp.concatenate([prev, cur, nxt], axis=0)
        acc = b_ref[:, cs]
        for k in range(D_CONV):
            tap = cur.astype(F32) if picks[k] is None else _dot(picks[k], ext)
            acc = acc + w_ref[k:k + 1, cs] * tap
        o_ref[:, cs] = _silu(acc).astype(o_ref.dtype)


def _conv_call(proj, conv_w, conv_b, layer, seq_len, col_block):
    t = proj.shape[0]
    cc = conv_w.shape[2]
    q = SSD_CHUNK
    nc = seq_len // q
    nseq = t // seq_len
    per = q // HALO
    last = t // HALO - 1
    return pl.pallas_call(
        functools.partial(_conv_kernel, nc=nc),
        grid=(nseq, nc),
        in_specs=[
            pl.BlockSpec((q, cc), lambda s, c: (s * nc + c, col_block)),
            pl.BlockSpec((HALO, cc), lambda s, c: (jnp.maximum((s * nc + c) * per - 1, 0), col_block)),
            pl.BlockSpec((HALO, cc), lambda s, c: (jnp.minimum((s * nc + c + 1) * per, last), col_block)),
            pl.BlockSpec((None, D_CONV, cc), lambda s, c: (layer, 0, 0)),
            pl.BlockSpec((None, 1, cc), lambda s, c: (layer, 0, 0)),
        ],
        out_specs=pl.BlockSpec((q, cc), lambda s, c: (s * nc + c, 0)),
        out_shape=jax.ShapeDtypeStruct((t, cc), BF16),
        compiler_params=_params("parallel", "arbitrary"),
        name="conv",
    )(proj, proj, proj, conv_w, conv_b)


def _ssd_factors(dtT_ref, dtbT_ref, alogT_ref, rev, heads):
    q = dtT_ref.shape[1]

    def dsel(fwd, bwd):
        return jnp.where(rev, bwd, fwd)

    dtT = _softplus(dsel(dtT_ref[0:heads, :], dtT_ref[heads:2 * heads, :])
                    + dsel(dtbT_ref[0:heads, :], dtbT_ref[heads:2 * heads, :]))
    aT = dtT * (-jnp.exp(dsel(alogT_ref[0:heads, :], alogT_ref[heads:2 * heads, :]))) * LOG2E
    ri = lax.broadcasted_iota(I32, (q, q), 0)
    ci = lax.broadcasted_iota(I32, (q, q), 1)
    triT = dsel(jnp.where(ci >= ri, 1.0, 0.0), jnp.where(ri >= ci, 1.0, 0.0)).astype(BF16)
    b1, b2, b3 = _split3(aT)
    a_csT = (_dot(b1, triT) + _dot(b2, triT)) + _dot(b3, triT)
    a_cs = a_csT.T
    tot = dsel(a_cs[q - 1:q, :], a_cs[0:1, :])
    stack = jnp.concatenate([dtT.T, jnp.exp2(a_cs), jnp.exp2(tot - a_cs)], axis=0)
    s_hi = stack.astype(BF16)
    s_lo = (stack - s_hi.astype(F32)).astype(BF16)
    return a_cs, a_csT, jnp.concatenate([s_hi, s_lo], axis=1)


def _ssd_kernel(xbc_ref, dtT_ref, dtTn_ref, z_ref, dtbT_ref, alogT_ref, dsk_ref, nw_ref,
                *rest, nc, heads, hdim, groups, nstate, has_h0, want_state):
    rest = list(rest)
    h0_ref = so_ref = None
    if has_h0:
        h0_ref = rest.pop(0)
    y_ref = rest.pop(0)
    if want_state:
        so_ref = rest.pop(0)
    st_ref, yf_ref, gz_ref, acs_ref, acsT_ref, scat_ref = rest

    q = SSD_CHUNK
    db = heads * hdim
    gn = groups * nstate
    hpg = heads // groups
    gw = hpg * hdim
    seq_len = nc * q
    t = pl.program_id(1)
    rev = t >= nc
    c = jnp.where(rev, 2 * nc - 1 - t, t)
    slot = t & 1

    def dsel(fwd, bwd):
        return jnp.where(rev, bwd, fwd)

    def stage(src_ref, is_rev, dst):
        a_cs, a_csT, s_cat = _ssd_factors(src_ref, dtbT_ref, alogT_ref, is_rev, heads)
        acs_ref[dst] = a_cs
        acsT_ref[dst] = a_csT
        scat_ref[dst] = s_cat

    @pl.when(t == 0)
    def _():
        stage(dtT_ref, rev, slot)

    @pl.when((t == 0) | (t == nc))
    def _():
        st_ref[...] = h0_ref[...] if has_h0 else jnp.zeros_like(st_ref)

    a_cs = acs_ref[slot]
    a_csT = acsT_ref[slot]
    s_cat = scat_ref[slot]

    ri = lax.broadcasted_iota(I32, (q, q), 0)
    ci = lax.broadcasted_iota(I32, (q, q), 1)
    mask = dsel(jnp.where(ri >= ci, 1.0, 0.0), jnp.where(ci >= ri, 1.0, 0.0)) > 0.5

    assert hdim & (hdim - 1) == 0 and heads & (heads - 1) == 0
    eh = lax.broadcasted_iota(I32, (2 * heads, db), 0) & (heads - 1)
    ec = lax.shift_right_logical(lax.broadcasted_iota(I32, (2 * heads, db), 1), hdim.bit_length() - 1)
    expand = jnp.where(eh == ec, 1.0, 0.0).astype(BF16)
    low_half = (lax.broadcasted_iota(I32, (q, gw), 1) & hdim) == 0

    row0 = pl.multiple_of(c * q, q)
    dst0 = pl.multiple_of(jnp.where(rev, seq_len, c * q), q)
    ssq = jnp.zeros((q, 2 * hdim), F32)

    for g in range(groups):
        gc = slice(g * gw, (g + 1) * gw)
        ex = _dot(s_cat, expand[:, gc])
        dt_x, e_x, dec_x = ex[0:q], ex[q:2 * q], ex[2 * q:3 * q]
        xg = xbc_ref[:, gc].astype(F32) * dt_x
        xg_lo = jnp.where(low_half, xg, 0.0).astype(BF16)
        xg_hi = jnp.where(low_half, 0.0, xg).astype(BF16)
        xdb = (xg * dec_x).astype(BF16)
        bg = xbc_ref[:, db + g * nstate:db + (g + 1) * nstate]
        cg = xbc_ref[:, db + gn + g * nstate:db + gn + (g + 1) * nstate]
        cb = lax.dot_general(cg, bg, (((1,), (1,)), ((), ())), preferred_element_type=F32)
        yo = _dot(cg, st_ref[:, gc].astype(BF16)) * e_x
        for pr in range(hpg // 2):
            h0 = g * hpg + 2 * pr
            ls = []
            for h in (h0, h0 + 1):
                d = a_cs[:, h:h + 1] - a_csT[h:h + 1, :]
                ls.append(jnp.where(mask, jnp.exp2(d), 0.0) * cb)
            lhs = jnp.concatenate(ls, axis=1).astype(BF16)
            pc = slice(2 * pr * hdim, (2 * pr + 2) * hdim)
            rhs = jnp.concatenate([xg_lo[:, pc], xg_hi[:, pc]], axis=0)
            yp = _dot(lhs, rhs) + yo[:, pc]
            cols = slice(h0 * hdim, (h0 + 2) * hdim)
            yf_ref[pl.ds(dst0, q), cols] = yp.astype(yf_ref.dtype)
            yt = ((yf_ref[pl.ds(row0, q), cols].astype(F32) + yp)
                  + dsk_ref[:, cols] * xbc_ref[:, cols].astype(F32))
            gz = yt * _silu(z_ref[:, cols].astype(F32))
            gz_ref[:, cols] = gz
            ssq = ssq + gz * gz
        new_state = lax.dot_general(bg, xdb, (((0,), (0,)), ((), ())), preferred_element_type=F32)
        st_ref[:, gc] = st_ref[:, gc] * dsel(e_x[q - 1:q, :], e_x[0:1, :]) + new_state

    ms = jnp.sum(ssq, axis=-1, keepdims=True) * (1.0 / db)
    y_ref[...] = (gz_ref[...] * lax.rsqrt(ms + EPS) * nw_ref[...]).astype(y_ref.dtype)

    stage(dtTn_ref, t + 1 >= nc, 1 - slot)

    if want_state:
        @pl.when((t == nc - 1) | (t == 2 * nc - 1))
        def _():
            so_ref[...] = st_ref[...].T


def _ssd_call(xbc, dtT, proj, z_block, dt_biasT, a_logT, d_skip_x, norm_w, layer, seq_len,
              heads, hdim, nstate, h0T=None, state_out=None):
    t, cc = xbc.shape
    q = SSD_CHUNK
    nc = seq_len // q
    nseq = t // seq_len
    db = heads * hdim
    h2 = 2 * heads
    has_h0 = h0T is not None
    want_state = state_out is not None
    n_even = dt_biasT.shape[0]

    def chunk(tt):
        return jnp.where(tt >= nc, 2 * nc - 1 - tt, tt)

    def late(tt):
        return jnp.where(tt >= nc, 2 * nc - 1 - tt, nc - 1)

    def direction(tt):
        return jnp.where(tt >= nc, 1, 0)

    in_specs = [
        pl.BlockSpec((q, cc), lambda s, tt: (s * nc + chunk(tt), 0)),
        pl.BlockSpec((h2, q), lambda s, tt: (0, s * nc + chunk(tt))),
        pl.BlockSpec((h2, q), lambda s, tt: (0, s * nc + chunk(jnp.minimum(tt + 1, 2 * nc - 1)))),
        pl.BlockSpec((q, db), lambda s, tt: (s * nc + late(tt), z_block)),
        pl.BlockSpec((None, h2, 1), lambda s, tt: (layer, 0, 0)),
        pl.BlockSpec((None, h2, 1), lambda s, tt: (layer, 0, 0)),
        pl.BlockSpec((None, 1, db), lambda s, tt: (layer, 0, 0)),
        pl.BlockSpec((None, 1, db), lambda s, tt: (layer, 0, 0)),
    ]
    args = [xbc, dtT, dtT, proj, dt_biasT, a_logT, d_skip_x, norm_w]
    if has_h0:
        in_specs.append(pl.BlockSpec((None, None, None, nstate, db),
                                     lambda s, tt: (s, layer, direction(tt), 0, 0)))
        args.append(h0T)
    out_specs = [pl.BlockSpec((q, db), lambda s, tt: (s * nc + late(tt), 0))]
    out_shape = [jax.ShapeDtypeStruct((t, db), BF16)]
    aliases = {}
    if want_state:
        out_specs.append(pl.BlockSpec((None, None, None, db, nstate),
                                      lambda s, tt: (s, layer, direction(tt), 0, 0)))
        out_shape.append(jax.ShapeDtypeStruct((nseq, n_even, 2, db, nstate), F32))
        if not isinstance(state_out, str):
            in_specs.append(pl.BlockSpec(memory_space=pl.ANY))
            args.append(state_out)
            aliases = {len(args) - 1: 1}
    kern = functools.partial(_ssd_kernel, nc=nc, heads=heads, hdim=hdim, groups=SSD_GROUPS, nstate=nstate,
                             has_h0=has_h0, want_state=want_state)
    if aliases:
        inner = kern
        n_in = len(args)

        def kern(*refs):
            return inner(*refs[:n_in - 1], *refs[n_in:])

    return pl.pallas_call(
        kern,
        grid=(nseq, 2 * nc),
        in_specs=in_specs,
        out_specs=out_specs,
        out_shape=out_shape,
        input_output_aliases=aliases,
        scratch_shapes=[
            pltpu.VMEM((nstate, db), F32),
            pltpu.VMEM((seq_len + q, db), BF16),
            pltpu.VMEM((q, db), F32),
            pltpu.VMEM((2, q, heads), F32),
            pltpu.VMEM((2, heads, q), F32),
            pltpu.VMEM((2, 3 * q, h2), BF16),
        ],
        compiler_params=_params("parallel", "arbitrary"),
        name="ssd",
    )(*args)


def _out_even_kernel(a_ref, b_ref, wa_ref, wb_ref, x_ref, mod_ref, o_ref):
    acc = _dot(a_ref[...], wa_ref[...]) + _dot(b_ref[...], wb_ref[...])
    o_ref[...] = x_ref[...] + mod_ref[2:3, :] * acc


def _out_even_call(a, b, w, x, mods, layer, wl, seq_len):
    t, d = x.shape
    d_a, d_b = a.shape[1], b.shape[1]
    assert d_a == d_b
    tm, cid = _row_tile(t, seq_len, mods.shape[1], 512)
    tn = _tile(d, 1024)
    return pl.pallas_call(
        _out_even_kernel,
        grid=(t // tm, d // tn),
        in_specs=[
            pl.BlockSpec((tm, d_a), lambda i, j: (i, 0)),
            pl.BlockSpec((tm, d_b), lambda i, j: (i, 0)),
            pl.BlockSpec((None, d_a, tn), lambda i, j: (wl, 0, j)),
            pl.BlockSpec((None, d_b, tn), lambda i, j: (wl, 1, j)),
            pl.BlockSpec((tm, tn), lambda i, j: (i, j)),
            pl.BlockSpec((None, None, N_MOD, tn), lambda i, j: (layer, cid(i), 0, j)),
        ],
        out_specs=pl.BlockSpec((tm, tn), lambda i, j: (i, j)),
        out_shape=jax.ShapeDtypeStruct((t, d), F32),
        compiler_params=_params("parallel", "parallel"),
        name="out_even",
    )(a, b, w, w, x, mods)


def _pool_kernel(hc_ref, pw_ref, ps_ref, wo_ref, x_ref, mod_ref, o_ref, *, win, sub):
    tm, d_c = hc_ref.shape
    ng = len(POOL_WINDOWS)
    pch = d_c // ng
    assert win & (win - 1) == 0
    ri = lax.broadcasted_iota(I32, (sub, sub), 0)
    ci = lax.broadcasted_iota(I32, (sub, sub), 1)
    same = lax.shift_right_logical(ri, win.bit_length() - 1) == lax.shift_right_logical(ci, win.bit_length() - 1)
    pos = lax.broadcasted_iota(I32, (sub, 1), 0) & (win - 1)
    mixed = []
    for g, k in enumerate(POOL_WINDOWS):
        off = ci - ri
        band = jnp.where(same & (off >= -(k // 2)) & (off < k - k // 2), 1.0, 0.0).astype(BF16)
        lo = jnp.maximum(pos - k // 2, 0)
        hi = jnp.minimum(pos - k // 2 + k, win)
        cnt = (hi - lo).astype(F32)
        pooled = []
        for r in range(tm // sub):
            xb = hc_ref[r * sub:(r + 1) * sub, g * pch:(g + 1) * pch]
            pooled.append(_dot(band, xb) / cnt - xb.astype(F32))
        pg = jnp.concatenate(pooled, axis=0).astype(BF16)
        mixed.append(_dot(pg, pw_ref[g]) * ps_ref[:, g * pch:(g + 1) * pch])
    mix = jnp.concatenate(mixed, axis=1).astype(BF16)
    o_ref[...] = x_ref[...] + mod_ref[2:3, :] * _dot(mix, wo_ref[...])


def _pool_call(hc, pool_w, pool_scale, w_out, x, mods, layer, wl, seq_len, win):
    t, d = x.shape
    d_c = hc.shape[1]
    _, ng, pch, _ = pool_w.shape
    sub = max(win, 128)
    assert sub % win == 0 and seq_len % sub == 0
    tm, cid = _row_tile(t, seq_len, mods.shape[1], 512)
    assert tm % sub == 0
    return pl.pallas_call(
        functools.partial(_pool_kernel, win=win, sub=sub),
        grid=(t // tm,),
        in_specs=[
            pl.BlockSpec((tm, d_c), lambda i: (i, 0)),
            pl.BlockSpec((None, ng, pch, pch), lambda i: (wl, 0, 0, 0)),
            pl.BlockSpec((None, 1, d_c), lambda i: (wl, 0, 0)),
            pl.BlockSpec((None, d_c, d), lambda i: (wl, 0, 0)),
            pl.BlockSpec((tm, d), lambda i: (i, 0)),
            pl.BlockSpec((None, None, N_MOD, d), lambda i: (layer, cid(i), 0, 0)),
        ],
        out_specs=pl.BlockSpec((tm, d), lambda i: (i, 0)),
        out_shape=jax.ShapeDtypeStruct((t, d), F32),
        compiler_params=_params("parallel"),
        name="pool_out",
    )(hc, pool_w, pool_scale, w_out, x, mods)


def _ffn_kernel(x_ref, mod_ref, nw_ref, w1_ref, w3_ref, w2_ref, fn_ref, o_ref, h_ref, *, nf, final, dn):
    f = pl.program_id(1)
    d = o_ref.shape[1]

    @pl.when(f == 0)
    def _():
        _norm_mod_rows(x_ref, nw_ref, mod_ref[3:4, :], mod_ref[4:5, :], h_ref, copy_ref=o_ref)

    h = h_ref[...]
    act = (_silu(_dot(h, w1_ref[...])) * _dot(h, w3_ref[...])).astype(BF16)
    for n0 in range(0, d, dn):
        o_ref[:, n0:n0 + dn] += mod_ref[5:6, n0:n0 + dn] * _dot(act, w2_ref[:, n0:n0 + dn])

    if final:
        @pl.when(f == nf - 1)
        def _():
            rows = min(NORM_ROWS, o_ref.shape[0])

            def body(r, carry):
                sl = pl.ds(pl.multiple_of(r * rows, rows), rows)
                y = o_ref[sl, :]
                ms = jnp.mean(y * y, axis=-1, keepdims=True)
                o_ref[sl, :] = y * lax.rsqrt(ms + EPS) * fn_ref[...]
                return carry

            lax.fori_loop(0, o_ref.shape[0] // rows, body, 0)


def _ffn_call(x, mods, norm_w, w1, w3, w2, final_nw, layer, seq_len, final):
    t, d = x.shape
    dff = w1.shape[2]
    tm, cid = _row_tile(t, seq_len, mods.shape[1], 1024)
    tf = _tile(dff, 512)
    nf = dff // tf
    x_spec = pl.BlockSpec((tm, d), lambda i, f: (i, 0), pipeline_mode=pl.Buffered(1))
    return pl.pallas_call(
        functools.partial(_ffn_kernel, nf=nf, final=final, dn=_tile(d, 512)),
        grid=(t // tm, nf),
        in_specs=[
            x_spec,
            pl.BlockSpec((None, None, N_MOD, d), lambda i, f: (layer, cid(i), 0, 0)),
            pl.BlockSpec((None, 1, d), lambda i, f: (layer, 0, 0)),
            pl.BlockSpec((None, d, tf), lambda i, f: (layer, 0, f)),
            pl.BlockSpec((None, d, tf), lambda i, f: (layer, 0, f)),
            pl.BlockSpec((None, tf, d), lambda i, f: (layer, f, 0)),
            pl.BlockSpec((1, d), lambda i, f: (0, 0)),
        ],
        out_specs=pl.BlockSpec((tm, d), lambda i, f: (i, 0)),
        out_shape=jax.ShapeDtypeStruct((t, d), F32),
        scratch_shapes=[pltpu.VMEM((tm, d), BF16)],
        compiler_params=_params("parallel", "arbitrary"),
        name="ffn",
    )(x, mods, norm_w, w1, w3, w2, final_nw)


def _run_group(x, seq_len, mods, win, h0T, want_state, p):
    depth = mods.shape[0]
    state = "new" if want_state else None
    for i in range(depth):
        if i % 2 == 0:
            e = i // 2
            proj, dtT = _in_call(x, mods, p["norm_mix"], p["w_in_even"], i, e, p["n_main"], seq_len, 1536,
                                 p["w_in_dtT"])
            a_out = _gmlp_call(proj, p["gmlp_ws"], p["gmlp_bsT"], p["gmlp_norm"], e, p["d_a"], seq_len)
            xbc = _conv_call(proj, p["ssd_conv_w"], p["ssd_conv_b"], e, seq_len, p["xbc_block"])
            res = _ssd_call(xbc, dtT, proj, p["z_block"], p["ssd_dt_biasT"], p["ssd_a_logT"], p["ssd_d_x"],
                            p["ssd_norm"], e, seq_len, p["heads"], p["hdim"], p["nstate"], h0T, state)
            if want_state:
                state = res[1]
            x = _out_even_call(a_out, res[0], p["w_out_even"], x, mods, i, e, seq_len)
        else:
            o = i // 2
            (hc,) = _in_call(x, mods, p["norm_mix"], p["w_in_odd"], i, o, p["w_in_odd"].shape[2], seq_len, 2048)
            x = _pool_call(hc, p["pool_w"], p["pool_scale"], p["w_out_odd"], x, mods, i, o, seq_len, win)
        x = _ffn_call(x, mods, p["norm_ffn"], p["ffn_w1"], p["ffn_w3"], p["ffn_w2"], p["final_norm"], i, seq_len,
                      final=(i == depth - 1))
    return x, state


def kernel(x_prompt, x_sample, state_ssd, c, c_ctx, w_mod, b_mod, norm_mix, norm_ffn, w_in_even, w_out_even, gmlp_norm, gmlp_ws, gmlp_bs, ssd_conv_w, ssd_conv_b, ssd_dt_bias, ssd_a_log, ssd_d, ssd_norm, w_in_odd, pool_w, pool_scale, w_out_odd, ffn_w1, ffn_w3, ffn_w2, final_norm):
    batch, seq, d = x_prompt.shape
    dec_batch, dec_seq, _ = x_sample.shape
    depth = w_mod.shape[0]
    n_even = w_in_even.shape[0]
    heads, hdim, nstate = state_ssd.shape[3:]
    d_b = heads * hdim
    d_a = gmlp_norm.shape[1]
    cc = ssd_conv_w.shape[2]
    n_main = 2 * d_a + d_b + cc
    h2 = 2 * heads
    assert w_in_even.shape[2] == n_main + h2 and h2 <= LANES
    assert (2 * d_a) % d_b == 0 and (n_main - cc) % cc == 0

    n_cond = 1 + dec_batch
    rows = -(-n_cond // 8) * 8
    cond = jnp.concatenate([c_ctx[None], c, jnp.zeros((rows - n_cond, d), F32)], axis=0)
    mods = _mod_call(cond, w_mod, b_mod).reshape(depth, rows, N_MOD, d)

    p = dict(
        d_a=d_a, heads=heads, hdim=hdim, nstate=nstate, n_main=n_main,
        z_block=(2 * d_a) // d_b, xbc_block=(n_main - cc) // cc,
        norm_mix=norm_mix.reshape(depth, 1, d), norm_ffn=norm_ffn.reshape(depth, 1, d),
        w_in_even=w_in_even.astype(BF16),
        w_in_dtT=jnp.swapaxes(w_in_even[:, :, n_main:], 1, 2).astype(BF16),
        w_out_even=w_out_even.astype(BF16),
        gmlp_norm=gmlp_norm.reshape(n_even, 1, d_a), gmlp_ws=gmlp_ws.astype(BF16),
        gmlp_bsT=jnp.swapaxes(gmlp_bs, 1, 2),
        ssd_conv_w=ssd_conv_w, ssd_conv_b=ssd_conv_b.reshape(n_even, 1, cc),
        ssd_dt_biasT=ssd_dt_bias.reshape(n_even, h2, 1), ssd_a_logT=ssd_a_log.reshape(n_even, h2, 1),
        ssd_d_x=jnp.repeat(ssd_d, hdim, axis=1).reshape(n_even, 1, d_b),
        ssd_norm=ssd_norm.reshape(n_even, 1, d_b),
        w_in_odd=w_in_odd.astype(BF16), pool_w=pool_w.astype(BF16),
        pool_scale=pool_scale.reshape(-1, 1, d), w_out_odd=w_out_odd.astype(BF16),
        ffn_w1=ffn_w1.astype(BF16), ffn_w3=ffn_w3.astype(BF16), ffn_w2=ffn_w2.astype(BF16),
        final_norm=final_norm.reshape(1, d),
    )

    y_p, state_new = _run_group(x_prompt.reshape(batch * seq, d), seq, mods[:, 0:1], seq, None, True, p)
    h0T = jnp.swapaxes(state_ssd.reshape(dec_batch, n_even, 2, d_b, nstate), 3, 4)
    y_s, _ = _run_group(x_sample.reshape(dec_batch * dec_seq, d), dec_seq, mods[:, 1:n_cond], GRID_W, h0T, False, p)

    return (y_p.reshape(batch, seq, d), y_s.reshape(dec_batch, dec_seq, d),
            state_new.reshape(batch, n_even, 2, heads, hdim, nstate))
```

```python
import functools
import math

import jax
import jax.numpy as jnp
from jax import lax
from jax.experimental import pallas as pl
from jax.experimental.pallas import tpu as pltpu

F32 = jnp.float32
BF16 = jnp.bfloat16
I32 = jnp.int32

EPS = 1e-6
N_MOD = 6
GRID_W = 64
CHUNK = 128
SSD_CHUNK = 128
SSD_GROUPS = 4
D_CONV = 5
POOL_WINDOWS = (2, 4, 8, 16)
LANES = 128
HALO = 16
NORM_ROWS = 32
NORM_UNROLL = 4
CONV_COLS = 512
LOG2E = 1.4426950408889634
VMEM_LIMIT_BYTES = 56 * 1024 * 1024


def _params(*sem):
    return pltpu.CompilerParams(dimension_semantics=sem, vmem_limit_bytes=VMEM_LIMIT_BYTES)


def _tile(n, pref):
    t = min(n, pref)
    assert n % t == 0, (n, pref)
    return t


def _row_tile(t, seq_len, n_cond, pref):
    if n_cond == 1:
        return _tile(t, pref), (lambda i: 0)
    tm = _tile(seq_len, min(pref, seq_len))
    return tm, (lambda i: (i * tm) // seq_len)


def _silu(x):
    return x * jax.nn.sigmoid(x)


def _softplus(x):
    return jnp.maximum(x, 0.0) + jnp.log1p(jnp.exp(-jnp.abs(x)))


def _split3(x):
    x1 = x.astype(BF16)
    r = x - x1.astype(F32)
    x2 = r.astype(BF16)
    x3 = (r - x2.astype(F32)).astype(BF16)
    return x1, x2, x3


def _dot(a, b):
    return jnp.dot(a, b, preferred_element_type=F32)


def _norm_mod_rows(x_ref, nw_ref, shift, scale, h_ref, copy_ref=None):
    gain = nw_ref[...] * (1.0 + scale)
    rows = min(NORM_ROWS, x_ref.shape[0])

    def body(r, carry):
        sl = pl.ds(pl.multiple_of(r * rows, rows), rows)
        x = x_ref[sl, :]
        ms = jnp.mean(x * x, axis=-1, keepdims=True)
        h_ref[sl, :] = (x * lax.rsqrt(ms + EPS) * gain + shift).astype(h_ref.dtype)
        if copy_ref is not None:
            copy_ref[sl, :] = x
        return carry

    lax.fori_loop(0, x_ref.shape[0] // rows, body, 0, unroll=NORM_UNROLL)


def _mod_kernel(c_ref, w_ref, b_ref, o_ref):
    s = _silu(c_ref[...])
    w = w_ref[...]
    s1 = s.astype(BF16)
    s2 = (s - s1.astype(F32)).astype(BF16)
    w1 = w.astype(BF16)
    w2 = (w - w1.astype(F32)).astype(BF16)
    o_ref[...] = (_dot(s1, w1) + _dot(s2, w1) + _dot(s1, w2)) + b_ref[...]


def _mod_call(cond, w_mod, b_mod):
    depth, d, n = w_mod.shape
    rows = cond.shape[0]
    tn = _tile(n, 1024)
    return pl.pallas_call(
        _mod_kernel,
        grid=(depth, n // tn),
        in_specs=[
            pl.BlockSpec((rows, d), lambda l, j: (0, 0)),
            pl.BlockSpec((None, d, tn), lambda l, j: (l, 0, j)),
            pl.BlockSpec((None, 1, tn), lambda l, j: (l, 0, j)),
        ],
        out_specs=pl.BlockSpec((None, rows, tn), lambda l, j: (l, 0, j)),
        out_shape=jax.ShapeDtypeStruct((depth, rows, n), F32),
        compiler_params=_params("parallel", "parallel"),
        name="mod",
    )(cond, w_mod, b_mod.reshape(depth, 1, n))


def _in_kernel(x_ref, mod_ref, nw_ref, w_ref, *rest, with_dt):
    if with_dt:
        wdtT_ref, o_ref, dtT_ref, h_ref = rest
    else:
        o_ref, h_ref = rest

    @pl.when(pl.program_id(1) == 0)
    def _():
        _norm_mod_rows(x_ref, nw_ref, mod_ref[0:1, :], mod_ref[1:2, :], h_ref)
        if with_dt:
            dtT_ref[...] = lax.dot_general(wdtT_ref[...], h_ref[...], (((1,), (1,)), ((), ())),
                                           preferred_element_type=F32)

    o_ref[...] = _dot(h_ref[...], w_ref[...]).astype(o_ref.dtype)


def _in_call(x, mods, norm_w, w, layer, wl, n, seq_len, tn_pref, wdtT=None):
    t, d = x.shape
    tm, cid = _row_tile(t, seq_len, mods.shape[1], 1024)
    tn = _tile(n, tn_pref)
    with_dt = wdtT is not None
    in_specs = [
        pl.BlockSpec((tm, d), lambda i, j: (i, 0)),
        pl.BlockSpec((None, None, N_MOD, d), lambda i, j: (layer, cid(i), 0, 0)),
        pl.BlockSpec((None, 1, d), lambda i, j: (layer, 0, 0)),
        pl.BlockSpec((None, d, tn), lambda i, j: (wl, 0, j)),
    ]
    out_specs = [pl.BlockSpec((tm, tn), lambda i, j: (i, j))]
    out_shape = [jax.ShapeDtypeStruct((t, n), BF16)]
    args = [x, mods, norm_w, w]
    if with_dt:
        h2 = wdtT.shape[1]
        in_specs.append(pl.BlockSpec((None, h2, d), lambda i, j: (wl, 0, 0)))
        out_specs.append(pl.BlockSpec((h2, tm), lambda i, j: (0, i)))
        out_shape.append(jax.ShapeDtypeStruct((h2, t), F32))
        args.append(wdtT)
    return pl.pallas_call(
        functools.partial(_in_kernel, with_dt=with_dt),
        grid=(t // tm, n // tn),
        in_specs=in_specs,
        out_specs=out_specs,
        out_shape=out_shape,
        scratch_shapes=[pltpu.VMEM((tm, d), BF16)],
        compiler_params=_params("parallel", "arbitrary"),
        name="in_proj",
    )(*args)


def _gelu_tanh(x):
    k = math.sqrt(2.0 / math.pi)
    a = -2.0 * k * 0.044715 * LOG2E
    b = -2.0 * k * LOG2E
    return x / (1.0 + jnp.exp2(x * (x * x * a + b)))


def _gmlp_kernel(u_ref, v_ref, ws_ref, bsT_ref, vn_ref, o_ref, *, groups, ch):
    def body(c, carry):
        rows = pl.ds(pl.multiple_of(c * CHUNK, CHUNK), CHUNK)
        for g in range(groups):
            sl = slice(g * ch, (g + 1) * ch)
            v = _gelu_tanh(v_ref[rows, sl].astype(F32))
            ms = jnp.mean(v * v, axis=-1, keepdims=True)
            vn = v * lax.rsqrt(ms + EPS) * vn_ref[:, sl]
            sv = _dot(ws_ref[g], vn.astype(BF16)) + bsT_ref[:, g:g + 1]
            u = _gelu_tanh(u_ref[rows, sl].astype(F32))
            o_ref[rows, sl] = (u * sv).astype(o_ref.dtype)
        return carry

    lax.fori_loop(0, u_ref.shape[0] // CHUNK, body, 0)


def _gmlp_call(proj, ws, bsT, vnorm, layer, d_a, seq_len):
    t = proj.shape[0]
    groups = ws.shape[1]
    tm = _tile(seq_len, 4 * CHUNK)
    return pl.pallas_call(
        functools.partial(_gmlp_kernel, groups=groups, ch=d_a // groups),
        grid=(t // tm,),
        in_specs=[
            pl.BlockSpec((tm, d_a), lambda i: (i, 0)),
            pl.BlockSpec((tm, d_a), lambda i: (i, 1)),
            pl.BlockSpec((None, groups, CHUNK, CHUNK), lambda i: (layer, 0, 0, 0)),
            pl.BlockSpec((None, CHUNK, groups), lambda i: (layer, 0, 0)),
            pl.BlockSpec((None, 1, d_a), lambda i: (layer, 0, 0)),
        ],
        out_specs=pl.BlockSpec((tm, d_a), lambda i: (i, 0)),
        out_shape=jax.ShapeDtypeStruct((t, d_a), BF16),
        compiler_params=_params("parallel"),
        name="gmlp",
    )(proj, proj, ws, bsT, vnorm)


def _conv_kernel(cur_ref, prev_ref, next_ref, w_ref, b_ref, o_ref, *, nc):
    c = pl.program_id(1)
    bq, cc = cur_ref.shape
    q = min(bq, SSD_CHUNK)
    rows = q + 2 * HALO
    ri = lax.broadcasted_iota(I32, (q, rows), 0)
    ci = lax.broadcasted_iota(I32, (q, rows), 1)
    picks = [None if k == D_CONV // 2 else
             jnp.where(ci == ri + (HALO + k - D_CONV // 2), 1.0, 0.0).astype(BF16) for k in range(D_CONV)]
    cw = min(CONV_COLS, cc)
    nsub = bq // q
    for n0 in range(0, cc, cw):
        cs = slice(n0, n0 + cw)
        prev = jnp.where(c > 0, prev_ref[:, cs], jnp.zeros((HALO, cw), BF16))
        nxt = jnp.where(c < nc - 1, next_ref[:, cs], jnp.zeros((HALO, cw), BF16))
        for u in range(nsub):
            cur = cur_ref[u * q:(u + 1) * q, cs]
            before = prev if u == 0 else cur_ref[u * q - HALO:u * q, cs]
            after = nxt if u == nsub - 1 else cur_ref[(u + 1) * q:(u + 1) * q + HALO, cs]
            ext = jnp.concatenate([before, cur, after], axis=0)
            acc = b_ref[:, cs]
            for k in range(D_CONV):
                tap = cur.astype(F32) if picks[k] is None else _dot(picks[k], ext)
                acc = acc + w_ref[k:k + 1, cs] * tap
            o_ref[u * q:(u + 1) * q, cs] = _silu(acc).astype(o_ref.dtype)


def _conv_call(proj, conv_w, conv_b, layer, seq_len, col_block):
    t = proj.shape[0]
    cc = conv_w.shape[2]
    q = _tile(seq_len, 2 * SSD_CHUNK)
    nc = seq_len // q
    nseq = t // seq_len
    per = q // HALO
    last = t // HALO - 1
    return pl.pallas_call(
        functools.partial(_conv_kernel, nc=nc),
        grid=(nseq, nc),
        in_specs=[
            pl.BlockSpec((q, cc), lambda s, c: (s * nc + c, col_block)),
            pl.BlockSpec((HALO, cc), lambda s, c: (jnp.maximum((s * nc + c) * per - 1, 0), col_block)),
            pl.BlockSpec((HALO, cc), lambda s, c: (jnp.minimum((s * nc + c + 1) * per, last), col_block)),
            pl.BlockSpec((None, D_CONV, cc), lambda s, c: (layer, 0, 0)),
            pl.BlockSpec((None, 1, cc), lambda s, c: (layer, 0, 0)),
        ],
        out_specs=pl.BlockSpec((q, cc), lambda s, c: (s * nc + c, 0)),
        out_shape=jax.ShapeDtypeStruct((t, cc), BF16),
        compiler_params=_params("parallel", "arbitrary"),
        name="conv",
    )(proj, proj, proj, conv_w, conv_b)


def _ssd_factors(dtT_ref, dtbT_ref, alogT_ref, rev, heads):
    q = dtT_ref.shape[1]

    def dsel(fwd, bwd):
        return jnp.where(rev, bwd, fwd)

    dtT = _softplus(dsel(dtT_ref[0:heads, :], dtT_ref[heads:2 * heads, :])
                    + dsel(dtbT_ref[0:heads, :], dtbT_ref[heads:2 * heads, :]))
    aT = dtT * (-jnp.exp(dsel(alogT_ref[0:heads, :], alogT_ref[heads:2 * heads, :]))) * LOG2E
    ri = lax.broadcasted_iota(I32, (q, q), 0)
    ci = lax.broadcasted_iota(I32, (q, q), 1)
    triT = dsel(jnp.where(ci >= ri, 1.0, 0.0), jnp.where(ri >= ci, 1.0, 0.0)).astype(BF16)
    b1, b2, b3 = _split3(aT)
    a_csT = (_dot(b1, triT) + _dot(b2, triT)) + _dot(b3, triT)
    a_cs = a_csT.T
    tot = dsel(a_cs[q - 1:q, :], a_cs[0:1, :])
    stack = jnp.concatenate([dtT.T, jnp.exp2(a_cs), jnp.exp2(tot - a_cs)], axis=0)
    s_hi = stack.astype(BF16)
    s_lo = (stack - s_hi.astype(F32)).astype(BF16)
    return a_cs, a_csT, jnp.concatenate([s_hi, s_lo], axis=1)


def _ssd_kernel(xbc_ref, dtT_ref, dtTn_ref, dtbT_ref, alogT_ref, dsk_ref,
                *rest, nc, heads, hdim, groups, nstate, has_h0, want_state):
    rest = list(rest)
    h0_ref = so_ref = None
    if has_h0:
        h0_ref = rest.pop(0)
    y_ref = rest.pop(0)
    if want_state:
        so_ref = rest.pop(0)
    st_ref, yf_ref, acs_ref, acsT_ref, scat_ref = rest

    q = SSD_CHUNK
    db = heads * hdim
    gn = groups * nstate
    hpg = heads // groups
    gw = hpg * hdim
    seq_len = nc * q
    t = pl.program_id(1)
    rev = t >= nc
    c = jnp.where(rev, 2 * nc - 1 - t, t)
    slot = t & 1

    def dsel(fwd, bwd):
        return jnp.where(rev, bwd, fwd)

    def stage(src_ref, is_rev, dst):
        a_cs, a_csT, s_cat = _ssd_factors(src_ref, dtbT_ref, alogT_ref, is_rev, heads)
        acs_ref[dst] = a_cs
        acsT_ref[dst] = a_csT
        scat_ref[dst] = s_cat

    @pl.when(t == 0)
    def _():
        stage(dtT_ref, rev, slot)

    @pl.when((t == 0) | (t == nc))
    def _():
        st_ref[...] = h0_ref[...] if has_h0 else jnp.zeros_like(st_ref)

    a_cs = acs_ref[slot]
    a_csT = acsT_ref[slot]
    s_cat = scat_ref[slot]

    ri = lax.broadcasted_iota(I32, (q, q), 0)
    ci = lax.broadcasted_iota(I32, (q, q), 1)
    mask = dsel(jnp.where(ri >= ci, 1.0, 0.0), jnp.where(ci >= ri, 1.0, 0.0)) > 0.5

    assert hdim & (hdim - 1) == 0 and heads & (heads - 1) == 0
    eh = lax.broadcasted_iota(I32, (2 * heads, db), 0) & (heads - 1)
    ec = lax.shift_right_logical(lax.broadcasted_iota(I32, (2 * heads, db), 1), hdim.bit_length() - 1)
    expand = jnp.where(eh == ec, 1.0, 0.0).astype(BF16)
    low_half = (lax.broadcasted_iota(I32, (q, gw), 1) & hdim) == 0

    row0 = pl.multiple_of(c * q, q)
    dst0 = pl.multiple_of(jnp.where(rev, seq_len, c * q), q)

    for g in range(groups):
        gc = slice(g * gw, (g + 1) * gw)
        ex = _dot(s_cat, expand[:, gc])
        dt_x, e_x, dec_x = ex[0:q], ex[q:2 * q], ex[2 * q:3 * q]
        xg = xbc_ref[:, gc].astype(F32) * dt_x
        xg_lo = jnp.where(low_half, xg, 0.0).astype(BF16)
        xg_hi = jnp.where(low_half, 0.0, xg).astype(BF16)
        xdb = (xg * dec_x).astype(BF16)
        bg = xbc_ref[:, db + g * nstate:db + (g + 1) * nstate]
        cg = xbc_ref[:, db + gn + g * nstate:db + gn + (g + 1) * nstate]
        cb = lax.dot_general(cg, bg, (((1,), (1,)), ((), ())), preferred_element_type=F32)
        yo = _dot(cg, st_ref[:, gc].astype(BF16)) * e_x
        for pr in range(hpg // 2):
            h0 = g * hpg + 2 * pr
            ls = []
            for h in (h0, h0 + 1):
                d = a_cs[:, h:h + 1] - a_csT[h:h + 1, :]
                ls.append(jnp.where(mask, jnp.exp2(d), 0.0) * cb)
            lhs = jnp.concatenate(ls, axis=1).astype(BF16)
            pc = slice(2 * pr * hdim, (2 * pr + 2) * hdim)
            rhs = jnp.concatenate([xg_lo[:, pc], xg_hi[:, pc]], axis=0)
            yp = _dot(lhs, rhs) + yo[:, pc]
            cols = slice(h0 * hdim, (h0 + 2) * hdim)
            yf_ref[pl.ds(dst0, q), cols] = yp.astype(yf_ref.dtype)
            y_ref[:, cols] = ((yf_ref[pl.ds(row0, q), cols].astype(F32) + yp)
                              + dsk_ref[:, cols] * xbc_ref[:, cols].astype(F32)).astype(y_ref.dtype)
        new_state = lax.dot_general(bg, xdb, (((0,), (0,)), ((), ())), preferred_element_type=F32)
        st_ref[:, gc] = st_ref[:, gc] * dsel(e_x[q - 1:q, :], e_x[0:1, :]) + new_state

    stage(dtTn_ref, t + 1 >= nc, 1 - slot)

    if want_state:
        @pl.when((t == nc - 1) | (t == 2 * nc - 1))
        def _():
            so_ref[...] = st_ref[...].T


def _ssd_call(xbc, dtT, dt_biasT, a_logT, d_skip_x, layer, seq_len, heads, hdim, nstate, h0T=None, state_out=None):
    t, cc = xbc.shape
    q = SSD_CHUNK
    nc = seq_len // q
    nseq = t // seq_len
    db = heads * hdim
    h2 = 2 * heads
    has_h0 = h0T is not None
    want_state = state_out is not None
    n_even = dt_biasT.shape[0]

    def chunk(tt):
        return jnp.where(tt >= nc, 2 * nc - 1 - tt, tt)

    def late(tt):
        return jnp.where(tt >= nc, 2 * nc - 1 - tt, nc - 1)

    def direction(tt):
        return jnp.where(tt >= nc, 1, 0)

    in_specs = [
        pl.BlockSpec((q, cc), lambda s, tt: (s * nc + chunk(tt), 0)),
        pl.BlockSpec((h2, q), lambda s, tt: (0, s * nc + chunk(tt))),
        pl.BlockSpec((h2, q), lambda s, tt: (0, s * nc + chunk(jnp.minimum(tt + 1, 2 * nc - 1)))),
        pl.BlockSpec((None, h2, 1), lambda s, tt: (layer, 0, 0)),
        pl.BlockSpec((None, h2, 1), lambda s, tt: (layer, 0, 0)),
        pl.BlockSpec((None, 1, db), lambda s, tt: (layer, 0, 0)),
    ]
    args = [xbc, dtT, dtT, dt_biasT, a_logT, d_skip_x]
    if has_h0:
        in_specs.append(pl.BlockSpec((None, None, None, nstate, db),
                                     lambda s, tt: (s, layer, direction(tt), 0, 0)))
        args.append(h0T)
    out_specs = [pl.BlockSpec((q, db), lambda s, tt: (s * nc + late(tt), 0))]
    out_shape = [jax.ShapeDtypeStruct((t, db), BF16)]
    aliases = {}
    if want_state:
        out_specs.append(pl.BlockSpec((None, None, None, db, nstate),
                                      lambda s, tt: (s, layer, direction(tt), 0, 0)))
        out_shape.append(jax.ShapeDtypeStruct((nseq, n_even, 2, db, nstate), F32))
        if not isinstance(state_out, str):
            in_specs.append(pl.BlockSpec(memory_space=pl.ANY))
            args.append(state_out)
            aliases = {len(args) - 1: 1}
    kern = functools.partial(_ssd_kernel, nc=nc, heads=heads, hdim=hdim, groups=SSD_GROUPS, nstate=nstate,
                             has_h0=has_h0, want_state=want_state)
    if aliases:
        inner = kern
        n_in = len(args)

        def kern(*refs):
            return inner(*refs[:n_in - 1], *refs[n_in:])

    return pl.pallas_call(
        kern,
        grid=(nseq, 2 * nc),
        in_specs=in_specs,
        out_specs=out_specs,
        out_shape=out_shape,
        input_output_aliases=aliases,
        scratch_shapes=[
            pltpu.VMEM((nstate, db), F32),
            pltpu.VMEM((seq_len + q, db), BF16),
            pltpu.VMEM((2, q, heads), F32),
            pltpu.VMEM((2, heads, q), F32),
            pltpu.VMEM((2, 3 * q, h2), BF16),
        ],
        compiler_params=_params("parallel", "arbitrary"),
        name="ssd",
    )(*args)


def _out_even_kernel(a_ref, ys_ref, z_ref, nw_ref, w_ref, x_ref, mod_ref, o_ref, b_ref, *, halves):
    tm, d_a = a_ref.shape
    acc = _dot(a_ref[...], w_ref[0:d_a, :])
    rows = tm // halves
    for r in range(halves):
        sl = slice(r * rows, (r + 1) * rows)
        gz = ys_ref[sl, :].astype(F32) * _silu(z_ref[sl, :].astype(F32))
        ms = jnp.mean(gz * gz, axis=-1, keepdims=True)
        b_ref[sl, :] = (gz * lax.rsqrt(ms + EPS) * nw_ref[...]).astype(BF16)
    acc = acc + _dot(b_ref[...], w_ref[d_a:, :])
    o_ref[...] = x_ref[...] + mod_ref[2:3, :] * acc


def _out_even_call(a, ysum, proj, z_block, norm_w, w, x, mods, layer, wl, seq_len):
    t, d = x.shape
    d_a, d_b = a.shape[1], ysum.shape[1]
    tm, cid = _row_tile(t, seq_len, mods.shape[1], 512)
    return pl.pallas_call(
        functools.partial(_out_even_kernel, halves=2 if tm % 32 == 0 else 1),
        grid=(t // tm,),
        in_specs=[
            pl.BlockSpec((tm, d_a), lambda i: (i, 0)),
            pl.BlockSpec((tm, d_b), lambda i: (i, 0)),
            pl.BlockSpec((tm, d_b), lambda i: (i, z_block)),
            pl.BlockSpec((None, 1, d_b), lambda i: (wl, 0, 0)),
            pl.BlockSpec((None, d_a + d_b, d), lambda i: (wl, 0, 0), pipeline_mode=pl.Buffered(1)),
            pl.BlockSpec((tm, d), lambda i: (i, 0)),
            pl.BlockSpec((None, None, N_MOD, d), lambda i: (layer, cid(i), 0, 0)),
        ],
        out_specs=pl.BlockSpec((tm, d), lambda i: (i, 0)),
        out_shape=jax.ShapeDtypeStruct((t, d), F32),
        scratch_shapes=[pltpu.VMEM((tm, d_b), BF16)],
        compiler_params=_params("parallel"),
        name="out_even",
    )(a, ysum, proj, norm_w, w, x, mods)


def _pool_kernel(hc_ref, pw_ref, ps_ref, wo_ref, x_ref, mod_ref, o_ref, *, win, sub):
    tm, d_c = hc_ref.shape
    ng = len(POOL_WINDOWS)
    pch = d_c // ng
    assert win & (win - 1) == 0
    ri = lax.broadcasted_iota(I32, (sub, sub), 0)
    ci = lax.broadcasted_iota(I32, (sub, sub), 1)
    same = lax.shift_right_logical(ri, win.bit_length() - 1) == lax.shift_right_logical(ci, win.bit_length() - 1)
    pos = lax.broadcasted_iota(I32, (sub, 1), 0) & (win - 1)
    mixed = []
    for g, k in enumerate(POOL_WINDOWS):
        off = ci - ri
        band = jnp.where(same & (off >= -(k // 2)) & (off < k - k // 2), 1.0, 0.0).astype(BF16)
        lo = jnp.maximum(pos - k // 2, 0)
        hi = jnp.minimum(pos - k // 2 + k, win)
        cnt = (hi - lo).astype(F32)
        pooled = []
        for r in range(tm // sub):
            xb = hc_ref[r * sub:(r + 1) * sub, g * pch:(g + 1) * pch]
            pooled.append(_dot(band, xb) / cnt - xb.astype(F32))
        pg = jnp.concatenate(pooled, axis=0).astype(BF16)
        mixed.append(_dot(pg, pw_ref[g]) * ps_ref[:, g * pch:(g + 1) * pch])
    mix = jnp.concatenate(mixed, axis=1).astype(BF16)
    o_ref[...] = x_ref[...] + mod_ref[2:3, :] * _dot(mix, wo_ref[...])


def _pool_call(hc, pool_w, pool_scale, w_out, x, mods, layer, wl, seq_len, win):
    t, d = x.shape
    d_c = hc.shape[1]
    _, ng, pch, _ = pool_w.shape
    sub = max(win, 128)
    assert sub % win == 0 and seq_len % sub == 0
    tm, cid = _row_tile(t, seq_len, mods.shape[1], 512)
    assert tm % sub == 0
    return pl.pallas_call(
        functools.partial(_pool_kernel, win=win, sub=sub),
        grid=(t // tm,),
        in_specs=[
            pl.BlockSpec((tm, d_c), lambda i: (i, 0)),
            pl.BlockSpec((None, ng, pch, pch), lambda i: (wl, 0, 0, 0)),
            pl.BlockSpec((None, 1, d_c), lambda i: (wl, 0, 0)),
            pl.BlockSpec((None, d_c, d), lambda i: (wl, 0, 0)),
            pl.BlockSpec((tm, d), lambda i: (i, 0)),
            pl.BlockSpec((None, None, N_MOD, d), lambda i: (layer, cid(i), 0, 0)),
        ],
        out_specs=pl.BlockSpec((tm, d), lambda i: (i, 0)),
        out_shape=jax.ShapeDtypeStruct((t, d), F32),
        compiler_params=_params("parallel"),
        name="pool_out",
    )(hc, pool_w, pool_scale, w_out, x, mods)


def _ffn_kernel(x_ref, mod_ref, nw_ref, w1_ref, w3_ref, w2_ref, fn_ref, o_ref, h_ref, *, nf, final, dn):
    f = pl.program_id(1)
    d = o_ref.shape[1]

    @pl.when(f == 0)
    def _():
        _norm_mod_rows(x_ref, nw_ref, mod_ref[3:4, :], mod_ref[4:5, :], h_ref, copy_ref=o_ref)

    h = h_ref[...]
    act = (_silu(_dot(h, w1_ref[...])) * _dot(h, w3_ref[...])).astype(BF16)
    for n0 in range(0, d, dn):
        o_ref[:, n0:n0 + dn] += mod_ref[5:6, n0:n0 + dn] * _dot(act, w2_ref[:, n0:n0 + dn])

    if final:
        @pl.when(f == nf - 1)
        def _():
            rows = min(NORM_ROWS, o_ref.shape[0])

            def body(r, carry):
                sl = pl.ds(pl.multiple_of(r * rows, rows), rows)
                y = o_ref[sl, :]
                ms = jnp.mean(y * y, axis=-1, keepdims=True)
                o_ref[sl, :] = y * lax.rsqrt(ms + EPS) * fn_ref[...]
                return carry

            lax.fori_loop(0, o_ref.shape[0] // rows, body, 0)


def _ffn_call(x, mods, norm_w, w1, w3, w2, final_nw, layer, seq_len, final):
    t, d = x.shape
    dff = w1.shape[2]
    tm, cid = _row_tile(t, seq_len, mods.shape[1], 1024)
    tf = _tile(dff, 512)
    nf = dff // tf
    x_spec = pl.BlockSpec((tm, d), lambda i, f: (i, 0))
    return pl.pallas_call(
        functools.partial(_ffn_kernel, nf=nf, final=final, dn=_tile(d, 512)),
        grid=(t // tm, nf),
        in_specs=[
            x_spec,
            pl.BlockSpec((None, None, N_MOD, d), lambda i, f: (layer, cid(i), 0, 0)),
            pl.BlockSpec((None, 1, d), lambda i, f: (layer, 0, 0)),
            pl.BlockSpec((None, d, tf), lambda i, f: (layer, 0, f)),
            pl.BlockSpec((None, d, tf), lambda i, f: (layer, 0, f)),
            pl.BlockSpec((None, tf, d), lambda i, f: (layer, f, 0)),
            pl.BlockSpec((1, d), lambda i, f: (0, 0)),
        ],
        out_specs=pl.BlockSpec((tm, d), lambda i, f: (i, 0)),
        out_shape=jax.ShapeDtypeStruct((t, d), F32),
        scratch_shapes=[pltpu.VMEM((tm, d), BF16)],
        compiler_params=_params("parallel", "arbitrary"),
        name="ffn",
    )(x, mods, norm_w, w1, w3, w2, final_nw)


def _run_group(x, seq_len, mods, win, h0T, want_state, p):
    depth = mods.shape[0]
    state = "new" if want_state else None
    for i in range(depth):
        if i % 2 == 0:
            e = i // 2
            proj, dtT = _in_call(x, mods, p["norm_mix"], p["w_in_even"], i, e, p["n_main"], seq_len, 1536,
                                 p["w_in_dtT"])
            a_out = _gmlp_call(proj, p["gmlp_ws"], p["gmlp_bsT"], p["gmlp_norm"], e, p["d_a"], seq_len)
            xbc = _conv_call(proj, p["ssd_conv_w"], p["ssd_conv_b"], e, seq_len, p["xbc_block"])
            res = _ssd_call(xbc, dtT, p["ssd_dt_biasT"], p["ssd_a_logT"], p["ssd_d_x"], e, seq_len,
                            p["heads"], p["hdim"], p["nstate"], h0T, state)
            if want_state:
                state = res[1]
            x = _out_even_call(a_out, res[0], proj, p["z_block"], p["ssd_norm"], p["w_out_even"], x, mods, i, e,
                               seq_len)
        else:
            o = i // 2
            (hc,) = _in_call(x, mods, p["norm_mix"], p["w_in_odd"], i, o, p["w_in_odd"].shape[2], seq_len, 2048)
            x = _pool_call(hc, p["pool_w"], p["pool_scale"], p["w_out_odd"], x, mods, i, o, seq_len, win)
        x = _ffn_call(x, mods, p["norm_ffn"], p["ffn_w1"], p["ffn_w3"], p["ffn_w2"], p["final_norm"], i, seq_len,
                      final=(i == depth - 1))
    return x, state


def kernel(x_prompt, x_sample, state_ssd, c, c_ctx, w_mod, b_mod, norm_mix, norm_ffn, w_in_even, w_out_even, gmlp_norm, gmlp_ws, gmlp_bs, ssd_conv_w, ssd_conv_b, ssd_dt_bias, ssd_a_log, ssd_d, ssd_norm, w_in_odd, pool_w, pool_scale, w_out_odd, ffn_w1, ffn_w3, ffn_w2, final_norm):
    batch, seq, d = x_prompt.shape
    dec_batch, dec_seq, _ = x_sample.shape
    depth = w_mod.shape[0]
    n_even = w_in_even.shape[0]
    heads, hdim, nstate = state_ssd.shape[3:]
    d_b = heads * hdim
    d_a = gmlp_norm.shape[1]
    cc = ssd_conv_w.shape[2]
    n_main = 2 * d_a + d_b + cc
    h2 = 2 * heads
    assert w_in_even.shape[2] == n_main + h2 and h2 <= LANES
    assert (2 * d_a) % d_b == 0 and (n_main - cc) % cc == 0

    n_cond = 1 + dec_batch
    rows = -(-n_cond // 8) * 8
    cond = jnp.concatenate([c_ctx[None], c, jnp.zeros((rows - n_cond, d), F32)], axis=0)
    mods = _mod_call(cond, w_mod, b_mod).reshape(depth, rows, N_MOD, d)

    p = dict(
        d_a=d_a, heads=heads, hdim=hdim, nstate=nstate, n_main=n_main,
        z_block=(2 * d_a) // d_b, xbc_block=(n_main - cc) // cc,
        norm_mix=norm_mix.reshape(depth, 1, d), norm_ffn=norm_ffn.reshape(depth, 1, d),
        w_in_even=w_in_even.astype(BF16),
        w_in_dtT=jnp.swapaxes(w_in_even[:, :, n_main:], 1, 2).astype(BF16),
        w_out_even=w_out_even.astype(BF16),
        gmlp_norm=gmlp_norm.reshape(n_even, 1, d_a), gmlp_ws=gmlp_ws.astype(BF16),
        gmlp_bsT=jnp.swapaxes(gmlp_bs, 1, 2),
        ssd_conv_w=ssd_conv_w, ssd_conv_b=ssd_conv_b.reshape(n_even, 1, cc),
        ssd_dt_biasT=ssd_dt_bias.reshape(n_even, h2, 1), ssd_a_logT=ssd_a_log.reshape(n_even, h2, 1),
        ssd_d_x=jnp.repeat(ssd_d, hdim, axis=1).reshape(n_even, 1, d_b),
        ssd_norm=ssd_norm.reshape(n_even, 1, d_b),
        w_in_odd=w_in_odd.astype(BF16), pool_w=pool_w.astype(BF16),
        pool_scale=pool_scale.reshape(-1, 1, d), w_out_odd=w_out_odd.astype(BF16),
        ffn_w1=ffn_w1.astype(BF16), ffn_w3=ffn_w3.astype(BF16), ffn_w2=ffn_w2.astype(BF16),
        final_norm=final_norm.reshape(1, d),
    )

    y_p, state_new = _run_group(x_prompt.reshape(batch * seq, d), seq, mods[:, 0:1], seq, None, True, p)
    h0T = jnp.swapaxes(state_ssd.reshape(dec_batch, n_even, 2, d_b, nstate), 3, 4)
    y_s, _ = _run_group(x_sample.reshape(dec_batch * dec_seq, d), dec_seq, mods[:, 1:n_cond], GRID_W, h0T, False, p)

    return (y_p.reshape(batch, seq, d), y_s.reshape(dec_batch, dec_seq, d),
            state_new.reshape(batch, n_even, 2, heads, hdim, nstate))
```

```python
import functools
import math

import jax
import jax.numpy as jnp
from jax import lax
from jax.experimental import pallas as pl
from jax.experimental.pallas import tpu as pltpu

F32 = jnp.float32
BF16 = jnp.bfloat16
I32 = jnp.int32

EPS = 1e-6
N_MOD = 6
GRID_W = 64
CHUNK = 128
SSD_CHUNK = 128
SSD_STEP_CHUNKS = 2
SSD_GROUPS = 4
D_CONV = 5
POOL_WINDOWS = (2, 4, 8, 16)
LANES = 128
HALO = 16
NORM_ROWS = 32
NORM_UNROLL = 4
CONV_COLS = 512
LOG2E = 1.4426950408889634
VMEM_LIMIT_BYTES = 56 * 1024 * 1024


def _params(*sem):
    return pltpu.CompilerParams(dimension_semantics=sem, vmem_limit_bytes=VMEM_LIMIT_BYTES)


def _tile(n, pref):
    t = min(n, pref)
    assert n % t == 0, (n, pref)
    return t


def _row_tile(t, seq_len, n_cond, pref):
    if n_cond == 1:
        return _tile(t, pref), (lambda i: 0)
    tm = _tile(seq_len, min(pref, seq_len))
    return tm, (lambda i: (i * tm) // seq_len)


def _silu(x):
    return x * jax.nn.sigmoid(x)


def _softplus(x):
    return jnp.maximum(x, 0.0) + jnp.log1p(jnp.exp(-jnp.abs(x)))


def _split3(x):
    x1 = x.astype(BF16)
    r = x - x1.astype(F32)
    x2 = r.astype(BF16)
    x3 = (r - x2.astype(F32)).astype(BF16)
    return x1, x2, x3


def _dot(a, b):
    return jnp.dot(a, b, preferred_element_type=F32)


def _norm_mod_rows(x_ref, nw_ref, shift, scale, h_ref, copy_ref=None):
    gain = nw_ref[...] * (1.0 + scale)
    rows = min(NORM_ROWS, x_ref.shape[0])

    def body(r, carry):
        sl = pl.ds(pl.multiple_of(r * rows, rows), rows)
        x = x_ref[sl, :]
        ms = jnp.mean(x * x, axis=-1, keepdims=True)
        h_ref[sl, :] = (x * lax.rsqrt(ms + EPS) * gain + shift).astype(h_ref.dtype)
        if copy_ref is not None:
            copy_ref[sl, :] = x
        return carry

    lax.fori_loop(0, x_ref.shape[0] // rows, body, 0, unroll=NORM_UNROLL)


def _mod_kernel(c_ref, w_ref, b_ref, o_ref):
    s = _silu(c_ref[...])
    w = w_ref[...]
    s1 = s.astype(BF16)
    s2 = (s - s1.astype(F32)).astype(BF16)
    w1 = w.astype(BF16)
    w2 = (w - w1.astype(F32)).astype(BF16)
    o_ref[...] = (_dot(s1, w1) + _dot(s2, w1) + _dot(s1, w2)) + b_ref[...]


def _mod_call(cond, w_mod, b_mod):
    depth, d, n = w_mod.shape
    rows = cond.shape[0]
    tn = _tile(n, 1024)
    return pl.pallas_call(
        _mod_kernel,
        grid=(depth, n // tn),
        in_specs=[
            pl.BlockSpec((rows, d), lambda l, j: (0, 0)),
            pl.BlockSpec((None, d, tn), lambda l, j: (l, 0, j)),
            pl.BlockSpec((None, 1, tn), lambda l, j: (l, 0, j)),
        ],
        out_specs=pl.BlockSpec((None, rows, tn), lambda l, j: (l, 0, j)),
        out_shape=jax.ShapeDtypeStruct((depth, rows, n), F32),
        compiler_params=_params("parallel", "parallel"),
        name="mod",
    )(cond, w_mod, b_mod.reshape(depth, 1, n))


def _in_kernel(x_ref, mod_ref, nw_ref, w_ref, *rest, with_dt):
    if with_dt:
        wdtT_ref, o_ref, dtT_ref, h_ref = rest
    else:
        o_ref, h_ref = rest

    @pl.when(pl.program_id(1) == 0)
    def _():
        _norm_mod_rows(x_ref, nw_ref, mod_ref[0:1, :], mod_ref[1:2, :], h_ref)
        if with_dt:
            dtT_ref[...] = lax.dot_general(wdtT_ref[...], h_ref[...], (((1,), (1,)), ((), ())),
                                           preferred_element_type=F32)

    o_ref[...] = _dot(h_ref[...], w_ref[...]).astype(o_ref.dtype)


def _in_call(x, mods, norm_w, w, layer, wl, n, seq_len, tn_pref, wdtT=None):
    t, d = x.shape
    tm, cid = _row_tile(t, seq_len, mods.shape[1], 1024)
    tn = _tile(n, tn_pref)
    with_dt = wdtT is not None
    in_specs = [
        pl.BlockSpec((tm, d), lambda i, j: (i, 0)),
        pl.BlockSpec((None, None, N_MOD, d), lambda i, j: (layer, cid(i), 0, 0)),
        pl.BlockSpec((None, 1, d), lambda i, j: (layer, 0, 0)),
        pl.BlockSpec((None, d, tn), lambda i, j: (wl, 0, j)),
    ]
    out_specs = [pl.BlockSpec((tm, tn), lambda i, j: (i, j))]
    out_shape = [jax.ShapeDtypeStruct((t, n), BF16)]
    args = [x, mods, norm_w, w]
    if with_dt:
        h2 = wdtT.shape[1]
        in_specs.append(pl.BlockSpec((None, h2, d), lambda i, j: (wl, 0, 0)))
        out_specs.append(pl.BlockSpec((h2, tm), lambda i, j: (0, i)))
        out_shape.append(jax.ShapeDtypeStruct((h2, t), F32))
        args.append(wdtT)
    return pl.pallas_call(
        functools.partial(_in_kernel, with_dt=with_dt),
        grid=(t // tm, n // tn),
        in_specs=in_specs,
        out_specs=out_specs,
        out_shape=out_shape,
        scratch_shapes=[pltpu.VMEM((tm, d), BF16)],
        compiler_params=_params("parallel", "arbitrary"),
        name="in_proj",
    )(*args)


def _gelu_tanh(x):
    k = math.sqrt(2.0 / math.pi)
    a = -2.0 * k * 0.044715 * LOG2E
    b = -2.0 * k * LOG2E
    return x / (1.0 + jnp.exp2(x * (x * x * a + b)))


def _conv_kernel(cur_ref, prev_ref, next_ref, w_ref, b_ref, o_ref, *, nc):
    c = pl.program_id(1)
    bq, cc = cur_ref.shape
    q = min(bq, SSD_CHUNK)
    rows = q + 2 * HALO
    ri = lax.broadcasted_iota(I32, (q, rows), 0)
    ci = lax.broadcasted_iota(I32, (q, rows), 1)
    picks = [None if k == D_CONV // 2 else
             jnp.where(ci == ri + (HALO + k - D_CONV // 2), 1.0, 0.0).astype(BF16) for k in range(D_CONV)]
    cw = min(CONV_COLS, cc)
    nsub = bq // q
    for n0 in range(0, cc, cw):
        cs = slice(n0, n0 + cw)
        prev = jnp.where(c > 0, prev_ref[:, cs], jnp.zeros((HALO, cw), BF16))
        nxt = jnp.where(c < nc - 1, next_ref[:, cs], jnp.zeros((HALO, cw), BF16))
        for u in range(nsub):
            cur = cur_ref[u * q:(u + 1) * q, cs]
            before = prev if u == 0 else cur_ref[u * q - HALO:u * q, cs]
            after = nxt if u == nsub - 1 else cur_ref[(u + 1) * q:(u + 1) * q + HALO, cs]
            ext = jnp.concatenate([before, cur, after], axis=0)
            acc = b_ref[:, cs]
            for k in range(D_CONV):
                tap = cur.astype(F32) if picks[k] is None else _dot(picks[k], ext)
                acc = acc + w_ref[k:k + 1, cs] * tap
            o_ref[u * q:(u + 1) * q, cs] = _silu(acc).astype(o_ref.dtype)


def _conv_call(proj, conv_w, conv_b, layer, seq_len, col_block):
    t = proj.shape[0]
    cc = conv_w.shape[2]
    q = _tile(seq_len, 2 * SSD_CHUNK)
    nc = seq_len // q
    nseq = t // seq_len
    per = q // HALO
    last = t // HALO - 1
    return pl.pallas_call(
        functools.partial(_conv_kernel, nc=nc),
        grid=(nseq, nc),
        in_specs=[
            pl.BlockSpec((q, cc), lambda s, c: (s * nc + c, col_block)),
            pl.BlockSpec((HALO, cc), lambda s, c: (jnp.maximum((s * nc + c) * per - 1, 0), col_block)),
            pl.BlockSpec((HALO, cc), lambda s, c: (jnp.minimum((s * nc + c + 1) * per, last), col_block)),
            pl.BlockSpec((None, D_CONV, cc), lambda s, c: (layer, 0, 0)),
            pl.BlockSpec((None, 1, cc), lambda s, c: (layer, 0, 0)),
        ],
        out_specs=pl.BlockSpec((q, cc), lambda s, c: (s * nc + c, 0)),
        out_shape=jax.ShapeDtypeStruct((t, cc), BF16),
        compiler_params=_params("parallel", "arbitrary"),
        name="conv",
    )(proj, proj, proj, conv_w, conv_b)


def _ssd_factors(dtT_ref, lanes, dtbT_ref, alogT_ref, rev, heads):
    q = lanes.stop - lanes.start

    def dsel(fwd, bwd):
        return jnp.where(rev, bwd, fwd)

    dtT = _softplus(dsel(dtT_ref[0:heads, lanes], dtT_ref[heads:2 * heads, lanes])
                    + dsel(dtbT_ref[0:heads, :], dtbT_ref[heads:2 * heads, :]))
    aT = dtT * (-jnp.exp(dsel(alogT_ref[0:heads, :], alogT_ref[heads:2 * heads, :]))) * LOG2E
    ri = lax.broadcasted_iota(I32, (q, q), 0)
    ci = lax.broadcasted_iota(I32, (q, q), 1)
    triT = dsel(jnp.where(ci >= ri, 1.0, 0.0), jnp.where(ri >= ci, 1.0, 0.0)).astype(BF16)
    b1, b2, b3 = _split3(aT)
    a_csT = (_dot(b1, triT) + _dot(b2, triT)) + _dot(b3, triT)
    a_cs = a_csT.T
    tot = dsel(a_cs[q - 1:q, :], a_cs[0:1, :])
    stack = jnp.concatenate([dtT.T, jnp.exp2(a_cs), jnp.exp2(tot - a_cs)], axis=0)
    s_hi = stack.astype(BF16)
    s_lo = (stack - s_hi.astype(F32)).astype(BF16)
    return a_cs, a_csT, jnp.concatenate([s_hi, s_lo], axis=1)


def _ssd_kernel(xbc_ref, dtT_ref, dtTn_ref, dtbT_ref, alogT_ref, dsk_ref,
                *rest, nc, heads, hdim, groups, nstate, has_h0, want_state):
    rest = list(rest)
    h0_ref = so_ref = None
    if has_h0:
        h0_ref = rest.pop(0)
    y_ref = rest.pop(0)
    if want_state:
        so_ref = rest.pop(0)
    st_ref, yf_ref, acs_ref, acsT_ref, scat_ref = rest

    q = SSD_CHUNK
    bq = xbc_ref.shape[0]
    nsub = bq // q
    db = heads * hdim
    gn = groups * nstate
    hpg = heads // groups
    gw = hpg * hdim
    seq_len = nc * bq
    t = pl.program_id(1)
    rev = t >= nc
    c = jnp.where(rev, 2 * nc - 1 - t, t)
    slot = t & 1

    def dsel(fwd, bwd):
        return jnp.where(rev, bwd, fwd)

    def stage(src_ref, is_rev, dst):
        for s in range(nsub):
            a_cs, a_csT, s_cat = _ssd_factors(src_ref, slice(s * q, (s + 1) * q), dtbT_ref, alogT_ref, is_rev,
                                              heads)
            acs_ref[dst, s] = a_cs
            acsT_ref[dst, s] = a_csT
            scat_ref[dst, s] = s_cat

    @pl.when(t == 0)
    def _():
        stage(dtT_ref, rev, slot)

    @pl.when((t == 0) | (t == nc))
    def _():
        st_ref[...] = h0_ref[...] if has_h0 else jnp.zeros_like(st_ref)

    ri = lax.broadcasted_iota(I32, (q, q), 0)
    ci = lax.broadcasted_iota(I32, (q, q), 1)
    mask = dsel(jnp.where(ri >= ci, 1.0, 0.0), jnp.where(ci >= ri, 1.0, 0.0)) > 0.5

    assert hdim & (hdim - 1) == 0 and heads & (heads - 1) == 0
    eh = lax.broadcasted_iota(I32, (2 * heads, db), 0) & (heads - 1)
    ec = lax.shift_right_logical(lax.broadcasted_iota(I32, (2 * heads, db), 1), hdim.bit_length() - 1)
    expand = jnp.where(eh == ec, 1.0, 0.0).astype(BF16)
    low_half = (lax.broadcasted_iota(I32, (q, gw), 1) & hdim) == 0

    for u in range(nsub):
        su = jnp.where(rev, nsub - 1 - u, u) if nsub > 1 else 0
        rows = pl.ds(pl.multiple_of(su * q, q), q) if nsub > 1 else slice(None)
        row0 = pl.multiple_of(c * bq + su * q, q)
        dst0 = pl.multiple_of(jnp.where(rev, seq_len, row0), q)
        a_cs = acs_ref[slot, su]
        a_csT = acsT_ref[slot, su]
        s_cat = scat_ref[slot, su]

        for g in range(groups):
            gc = slice(g * gw, (g + 1) * gw)
            ex = _dot(s_cat, expand[:, gc])
            dt_x, e_x, dec_x = ex[0:q], ex[q:2 * q], ex[2 * q:3 * q]
            xg = xbc_ref[rows, gc].astype(F32) * dt_x
            xg_lo = jnp.where(low_half, xg, 0.0).astype(BF16)
            xg_hi = jnp.where(low_half, 0.0, xg).astype(BF16)
            xdb = (xg * dec_x).astype(BF16)
            bg = xbc_ref[rows, db + g * nstate:db + (g + 1) * nstate]
            cg = xbc_ref[rows, db + gn + g * nstate:db + gn + (g + 1) * nstate]
            cb = lax.dot_general(cg, bg, (((1,), (1,)), ((), ())), preferred_element_type=F32)
            yo = _dot(cg, st_ref[:, gc].astype(BF16)) * e_x
            for pr in range(hpg // 2):
                h0 = g * hpg + 2 * pr
                ls = []
                for h in (h0, h0 + 1):
                    d = a_cs[:, h:h + 1] - a_csT[h:h + 1, :]
                    ls.append(jnp.where(mask, jnp.exp2(d), 0.0) * cb)
                lhs = jnp.concatenate(ls, axis=1).astype(BF16)
                pc = slice(2 * pr * hdim, (2 * pr + 2) * hdim)
                rhs = jnp.concatenate([xg_lo[:, pc], xg_hi[:, pc]], axis=0)
                yp = _dot(lhs, rhs) + yo[:, pc]
                cols = slice(h0 * hdim, (h0 + 2) * hdim)
                yf_ref[pl.ds(dst0, q), cols] = yp.astype(yf_ref.dtype)
                y_ref[rows, cols] = ((yf_ref[pl.ds(row0, q), cols].astype(F32) + yp)
                                     + dsk_ref[:, cols] * xbc_ref[rows, cols].astype(F32)).astype(y_ref.dtype)
            new_state = lax.dot_general(bg, xdb, (((0,), (0,)), ((), ())), preferred_element_type=F32)
            st_ref[:, gc] = st_ref[:, gc] * dsel(e_x[q - 1:q, :], e_x[0:1, :]) + new_state

    stage(dtTn_ref, t + 1 >= nc, 1 - slot)

    if want_state:
        @pl.when((t == nc - 1) | (t == 2 * nc - 1))
        def _():
            so_ref[...] = st_ref[...].T


def _ssd_call(xbc, dtT, dt_biasT, a_logT, d_skip_x, layer, seq_len, heads, hdim, nstate, h0T=None, state_out=None):
    t, cc = xbc.shape
    q = SSD_CHUNK
    bq = _tile(seq_len, SSD_STEP_CHUNKS * q)
    nsub = bq // q
    nc = seq_len // bq
    nseq = t // seq_len
    db = heads * hdim
    h2 = 2 * heads
    has_h0 = h0T is not None
    want_state = state_out is not None
    n_even = dt_biasT.shape[0]

    def chunk(tt):
        return jnp.where(tt >= nc, 2 * nc - 1 - tt, tt)

    def late(tt):
        return jnp.where(tt >= nc, 2 * nc - 1 - tt, nc - 1)

    def direction(tt):
        return jnp.where(tt >= nc, 1, 0)

    in_specs = [
        pl.BlockSpec((bq, cc), lambda s, tt: (s * nc + chunk(tt), 0)),
        pl.BlockSpec((h2, bq), lambda s, tt: (0, s * nc + chunk(tt))),
        pl.BlockSpec((h2, bq), lambda s, tt: (0, s * nc + chunk(jnp.minimum(tt + 1, 2 * nc - 1)))),
        pl.BlockSpec((None, h2, 1), lambda s, tt: (layer, 0, 0)),
        pl.BlockSpec((None, h2, 1), lambda s, tt: (layer, 0, 0)),
        pl.BlockSpec((None, 1, db), lambda s, tt: (layer, 0, 0)),
    ]
    args = [xbc, dtT, dtT, dt_biasT, a_logT, d_skip_x]
    if has_h0:
        in_specs.append(pl.BlockSpec((None, None, None, nstate, db),
                                     lambda s, tt: (s, layer, direction(tt), 0, 0)))
        args.append(h0T)
    out_specs = [pl.BlockSpec((bq, db), lambda s, tt: (s * nc + late(tt), 0))]
    out_shape = [jax.ShapeDtypeStruct((t, db), BF16)]
    aliases = {}
    if want_state:
        out_specs.append(pl.BlockSpec((None, None, None, db, nstate),
                                      lambda s, tt: (s, layer, direction(tt), 0, 0)))
        out_shape.append(jax.ShapeDtypeStruct((nseq, n_even, 2, db, nstate), F32))
        if not isinstance(state_out, str):
            in_specs.append(pl.BlockSpec(memory_space=pl.ANY))
            args.append(state_out)
            aliases = {len(args) - 1: 1}
    kern = functools.partial(_ssd_kernel, nc=nc, heads=heads, hdim=hdim, groups=SSD_GROUPS, nstate=nstate,
                             has_h0=has_h0, want_state=want_state)
    if aliases:
        inner = kern
        n_in = len(args)

        def kern(*refs):
            return inner(*refs[:n_in - 1], *refs[n_in:])

    return pl.pallas_call(
        kern,
        grid=(nseq, 2 * nc),
        in_specs=in_specs,
        out_specs=out_specs,
        out_shape=out_shape,
        input_output_aliases=aliases,
        scratch_shapes=[
            pltpu.VMEM((nstate, db), F32),
            pltpu.VMEM((seq_len + q, db), BF16),
            pltpu.VMEM((2, nsub, q, heads), F32),
            pltpu.VMEM((2, nsub, heads, q), F32),
            pltpu.VMEM((2, nsub, 3 * q, h2), BF16),
        ],
        compiler_params=_params("parallel", "arbitrary"),
        name="ssd",
    )(*args)


def _mix_even_kernel(u_ref, v_ref, z_ref, ys_ref, ws_ref, bsT_ref, vn_ref, nw_ref, w_ref, x_ref, mod_ref, o_ref,
                     b_ref, *, groups):
    tm, d_a = u_ref.shape
    d_b = ys_ref.shape[1]
    ch = d_a // groups
    assert d_a == d_b
    gz = ys_ref[...].astype(F32) * _silu(z_ref[...].astype(F32))
    ms = jnp.mean(gz * gz, axis=-1, keepdims=True)
    b_ref[...] = (gz * lax.rsqrt(ms + EPS) * nw_ref[...]).astype(BF16)
    acc = None
    for g in range(groups):
        sl = slice(g * ch, (g + 1) * ch)
        part = _dot(b_ref[:, sl], w_ref[d_a + g * ch:d_a + (g + 1) * ch, :])
        acc = part if acc is None else acc + part
        parts = []
        for c0 in range(0, tm, CHUNK):
            rows = slice(c0, c0 + CHUNK)
            v = _gelu_tanh(v_ref[rows, sl].astype(F32))
            ms = jnp.mean(v * v, axis=-1, keepdims=True)
            vn = v * lax.rsqrt(ms + EPS) * vn_ref[:, sl]
            sv = _dot(ws_ref[g], vn.astype(BF16)) + bsT_ref[:, g:g + 1]
            parts.append((_gelu_tanh(u_ref[rows, sl].astype(F32)) * sv).astype(BF16))
        acc = acc + _dot(jnp.concatenate(parts, axis=0), w_ref[sl, :])
    o_ref[...] = x_ref[...] + mod_ref[2:3, :] * acc


def _mix_even_call(proj, ysum, z_block, ws, bsT, vnorm, norm_w, w, x, mods, layer, wl, seq_len):
    t, d = x.shape
    d_b = ysum.shape[1]
    d_a = w.shape[1] - d_b
    groups = ws.shape[1]
    assert d_a == d_b
    tm, cid = _row_tile(t, seq_len, mods.shape[1], 2 * CHUNK)
    return pl.pallas_call(
        functools.partial(_mix_even_kernel, groups=groups),
        grid=(t // tm,),
        in_specs=[
            pl.BlockSpec((tm, d_a), lambda i: (i, 0)),
            pl.BlockSpec((tm, d_a), lambda i: (i, 1)),
            pl.BlockSpec((tm, d_b), lambda i: (i, z_block)),
            pl.BlockSpec((tm, d_b), lambda i: (i, 0)),
            pl.BlockSpec((None, groups, CHUNK, CHUNK), lambda i: (wl, 0, 0, 0)),
            pl.BlockSpec((None, CHUNK, groups), lambda i: (wl, 0, 0)),
            pl.BlockSpec((None, 1, d_a), lambda i: (wl, 0, 0)),
            pl.BlockSpec((None, 1, d_b), lambda i: (wl, 0, 0)),
            pl.BlockSpec((None, d_a + d_b, d), lambda i: (wl, 0, 0), pipeline_mode=pl.Buffered(1)),
            pl.BlockSpec((tm, d), lambda i: (i, 0)),
            pl.BlockSpec((None, None, N_MOD, d), lambda i: (layer, cid(i), 0, 0)),
        ],
        out_specs=pl.BlockSpec((tm, d), lambda i: (i, 0)),
        out_shape=jax.ShapeDtypeStruct((t, d), F32),
        scratch_shapes=[pltpu.VMEM((tm, d_b), BF16)],
        compiler_params=_params("parallel"),
        name="mix_even",
    )(proj, proj, proj, ysum, ws, bsT, vnorm, norm_w, w, x, mods)


def _pool_kernel(hc_ref, pw_ref, ps_ref, wo_ref, x_ref, mod_ref, o_ref, *, win, sub):
    tm, d_c = hc_ref.shape
    ng = len(POOL_WINDOWS)
    pch = d_c // ng
    assert win & (win - 1) == 0
    ri = lax.broadcasted_iota(I32, (sub, sub), 0)
    ci = lax.broadcasted_iota(I32, (sub, sub), 1)
    same = lax.shift_right_logical(ri, win.bit_length() - 1) == lax.shift_right_logical(ci, win.bit_length() - 1)
    pos = lax.broadcasted_iota(I32, (sub, 1), 0) & (win - 1)
    mixed = []
    for g, k in enumerate(POOL_WINDOWS):
        off = ci - ri
        band = jnp.where(same & (off >= -(k // 2)) & (off < k - k // 2), 1.0, 0.0).astype(BF16)
        lo = jnp.maximum(pos - k // 2, 0)
        hi = jnp.minimum(pos - k // 2 + k, win)
        cnt = (hi - lo).astype(F32)
        pooled = []
        for r in range(tm // sub):
            xb = hc_ref[r * sub:(r + 1) * sub, g * pch:(g + 1) * pch]
            pooled.append(_dot(band, xb) / cnt - xb.astype(F32))
        pg = jnp.concatenate(pooled, axis=0).astype(BF16)
        mixed.append(_dot(pg, pw_ref[g]) * ps_ref[:, g * pch:(g + 1) * pch])
    mix = jnp.concatenate(mixed, axis=1).astype(BF16)
    o_ref[...] = x_ref[...] + mod_ref[2:3, :] * _dot(mix, wo_ref[...])


def _pool_call(hc, pool_w, pool_scale, w_out, x, mods, layer, wl, seq_len, win):
    t, d = x.shape
    d_c = hc.shape[1]
    _, ng, pch, _ = pool_w.shape
    sub = max(win, 128)
    assert sub % win == 0 and seq_len % sub == 0
    tm, cid = _row_tile(t, seq_len, mods.shape[1], 512)
    assert tm % sub == 0
    return pl.pallas_call(
        functools.partial(_pool_kernel, win=win, sub=sub),
        grid=(t // tm,),
        in_specs=[
            pl.BlockSpec((tm, d_c), lambda i: (i, 0)),
            pl.BlockSpec((None, ng, pch, pch), lambda i: (wl, 0, 0, 0)),
            pl.BlockSpec((None, 1, d_c), lambda i: (wl, 0, 0)),
            pl.BlockSpec((None, d_c, d), lambda i: (wl, 0, 0)),
            pl.BlockSpec((tm, d), lambda i: (i, 0)),
            pl.BlockSpec((None, None, N_MOD, d), lambda i: (layer, cid(i), 0, 0)),
        ],
        out_specs=pl.BlockSpec((tm, d), lambda i: (i, 0)),
        out_shape=jax.ShapeDtypeStruct((t, d), F32),
        compiler_params=_params("parallel"),
        name="pool_out",
    )(hc, pool_w, pool_scale, w_out, x, mods)


def _ffn_kernel(x_ref, mod_ref, nw_ref, w1_ref, w3_ref, w2_ref, fn_ref, o_ref, h_ref, *, nf, final, dn):
    f = pl.program_id(1)
    d = o_ref.shape[1]

    @pl.when(f == 0)
    def _():
        _norm_mod_rows(x_ref, nw_ref, mod_ref[3:4, :], mod_ref[4:5, :], h_ref, copy_ref=o_ref)

    h = h_ref[...]
    act = (_silu(_dot(h, w1_ref[...])) * _dot(h, w3_ref[...])).astype(BF16)
    for n0 in range(0, d, dn):
        o_ref[:, n0:n0 + dn] += mod_ref[5:6, n0:n0 + dn] * _dot(act, w2_ref[:, n0:n0 + dn])

    if final:
        @pl.when(f == nf - 1)
        def _():
            rows = min(NORM_ROWS, o_ref.shape[0])

            def body(r, carry):
                sl = pl.ds(pl.multiple_of(r * rows, rows), rows)
                y = o_ref[sl, :]
                ms = jnp.mean(y * y, axis=-1, keepdims=True)
                o_ref[sl, :] = y * lax.rsqrt(ms + EPS) * fn_ref[...]
                return carry

            lax.fori_loop(0, o_ref.shape[0] // rows, body, 0)


def _ffn_call(x, mods, norm_w, w1, w3, w2, final_nw, layer, seq_len, final):
    t, d = x.shape
    dff = w1.shape[2]
    tm, cid = _row_tile(t, seq_len, mods.shape[1], 1024)
    tf = _tile(dff, 512)
    nf = dff // tf
    x_spec = pl.BlockSpec((tm, d), lambda i, f: (i, 0))
    return pl.pallas_call(
        functools.partial(_ffn_kernel, nf=nf, final=final, dn=_tile(d, 512)),
        grid=(t // tm, nf),
        in_specs=[
            x_spec,
            pl.BlockSpec((None, None, N_MOD, d), lambda i, f: (layer, cid(i), 0, 0)),
            pl.BlockSpec((None, 1, d), lambda i, f: (layer, 0, 0)),
            pl.BlockSpec((None, d, tf), lambda i, f: (layer, 0, f)),
            pl.BlockSpec((None, d, tf), lambda i, f: (layer, 0, f)),
            pl.BlockSpec((None, tf, d), lambda i, f: (layer, f, 0)),
            pl.BlockSpec((1, d), lambda i, f: (0, 0)),
        ],
        out_specs=pl.BlockSpec((tm, d), lambda i, f: (i, 0)),
        out_shape=jax.ShapeDtypeStruct((t, d), F32),
        scratch_shapes=[pltpu.VMEM((tm, d), BF16)],
        compiler_params=_params("parallel", "arbitrary"),
        name="ffn",
    )(x, mods, norm_w, w1, w3, w2, final_nw)


def _run_group(x, seq_len, mods, win, h0T, want_state, p):
    depth = mods.shape[0]
    state = "new" if want_state else None
    for i in range(depth):
        if i % 2 == 0:
            e = i // 2
            proj, dtT = _in_call(x, mods, p["norm_mix"], p["w_in_even"], i, e, p["n_main"], seq_len, 2304,
                                 p["w_in_dtT"])
            xbc = _conv_call(proj, p["ssd_conv_w"], p["ssd_conv_b"], e, seq_len, p["xbc_block"])
            res = _ssd_call(xbc, dtT, p["ssd_dt_biasT"], p["ssd_a_logT"], p["ssd_d_x"], e, seq_len,
                            p["heads"], p["hdim"], p["nstate"], h0T, state)
            if want_state:
                state = res[1]
            x = _mix_even_call(proj, res[0], p["z_block"], p["gmlp_ws"], p["gmlp_bsT"], p["gmlp_norm"],
                               p["ssd_norm"], p["w_out_even"], x, mods, i, e, seq_len)
        else:
            o = i // 2
            (hc,) = _in_call(x, mods, p["norm_mix"], p["w_in_odd"], i, o, p["w_in_odd"].shape[2], seq_len, 2048)
            x = _pool_call(hc, p["pool_w"], p["pool_scale"], p["w_out_odd"], x, mods, i, o, seq_len, win)
        x = _ffn_call(x, mods, p["norm_ffn"], p["ffn_w1"], p["ffn_w3"], p["ffn_w2"], p["final_norm"], i, seq_len,
                      final=(i == depth - 1))
    return x, state


def kernel(x_prompt, x_sample, state_ssd, c, c_ctx, w_mod, b_mod, norm_mix, norm_ffn, w_in_even, w_out_even, gmlp_norm, gmlp_ws, gmlp_bs, ssd_conv_w, ssd_conv_b, ssd_dt_bias, ssd_a_log, ssd_d, ssd_norm, w_in_odd, pool_w, pool_scale, w_out_odd, ffn_w1, ffn_w3, ffn_w2, final_norm):
    batch, seq, d = x_prompt.shape
    dec_batch, dec_seq, _ = x_sample.shape
    depth = w_mod.shape[0]
    n_even = w_in_even.shape[0]
    heads, hdim, nstate = state_ssd.shape[3:]
    d_b = heads * hdim
    d_a = gmlp_norm.shape[1]
    cc = ssd_conv_w.shape[2]
    n_main = 2 * d_a + d_b + cc
    h2 = 2 * heads
    assert w_in_even.shape[2] == n_main + h2 and h2 <= LANES
    assert (2 * d_a) % d_b == 0 and (n_main - cc) % cc == 0

    n_cond = 1 + dec_batch
    rows = -(-n_cond // 8) * 8
    cond = jnp.concatenate([c_ctx[None], c, jnp.zeros((rows - n_cond, d), F32)], axis=0)
    mods = _mod_call(cond, w_mod, b_mod).reshape(depth, rows, N_MOD, d)

    p = dict(
        heads=heads, hdim=hdim, nstate=nstate, n_main=n_main,
        z_block=(2 * d_a) // d_b, xbc_block=(n_main - cc) // cc,
        norm_mix=norm_mix.reshape(depth, 1, d), norm_ffn=norm_ffn.reshape(depth, 1, d),
        w_in_even=w_in_even.astype(BF16),
        w_in_dtT=jnp.swapaxes(w_in_even[:, :, n_main:], 1, 2).astype(BF16),
        w_out_even=w_out_even.astype(BF16),
        gmlp_norm=gmlp_norm.reshape(n_even, 1, d_a), gmlp_ws=gmlp_ws.astype(BF16),
        gmlp_bsT=jnp.swapaxes(gmlp_bs, 1, 2),
        ssd_conv_w=ssd_conv_w, ssd_conv_b=ssd_conv_b.reshape(n_even, 1, cc),
        ssd_dt_biasT=ssd_dt_bias.reshape(n_even, h2, 1), ssd_a_logT=ssd_a_log.reshape(n_even, h2, 1),
        ssd_d_x=jnp.repeat(ssd_d, hdim, axis=1).reshape(n_even, 1, d_b),
        ssd_norm=ssd_norm.reshape(n_even, 1, d_b),
        w_in_odd=w_in_odd.astype(BF16), pool_w=pool_w.astype(BF16),
        pool_scale=pool_scale.reshape(-1, 1, d), w_out_odd=w_out_odd.astype(BF16),
        ffn_w1=ffn_w1.astype(BF16), ffn_w3=ffn_w3.astype(BF16), ffn_w2=ffn_w2.astype(BF16),
        final_norm=final_norm.reshape(1, d),
    )

    y_p, state_new = _run_group(x_prompt.reshape(batch * seq, d), seq, mods[:, 0:1], seq, None, True, p)
    h0T = jnp.swapaxes(state_ssd.reshape(dec_batch, n_even, 2, d_b, nstate), 3, 4)
    y_s, _ = _run_group(x_sample.reshape(dec_batch * dec_seq, d), dec_seq, mods[:, 1:n_cond], GRID_W, h0T, False, p)

    return (y_p.reshape(batch, seq, d), y_s.reshape(dec_batch, dec_seq, d),
            state_new.reshape(batch, n_even, 2, heads, hdim, nstate))
```

```python
import functools
import math

import jax
import jax.numpy as jnp
from jax import lax
from jax.experimental import pallas as pl
from jax.experimental.pallas import tpu as pltpu

F32 = jnp.float32
BF16 = jnp.bfloat16
I32 = jnp.int32

EPS = 1e-6
N_MOD = 6
GRID_W = 64
CHUNK = 128
SSD_CHUNK = 128
SSD_STEP_CHUNKS = 4
SSD_GROUPS = 4
D_CONV = 5
POOL_WINDOWS = (2, 4, 8, 16)
LANES = 128
HALO = 16
NORM_ROWS = 32
CONV_COLS = 512
LOG2E = 1.4426950408889634
VMEM_LIMIT_BYTES = 56 * 1024 * 1024


def _params(*sem):
    return pltpu.CompilerParams(dimension_semantics=sem, vmem_limit_bytes=VMEM_LIMIT_BYTES)


def _tile(n, pref):
    t = min(n, pref)
    assert n % t == 0, (n, pref)
    return t


def _row_tile(t, seq_len, n_cond, pref):
    if n_cond == 1:
        return _tile(t, pref), (lambda i: 0)
    tm = _tile(seq_len, min(pref, seq_len))
    return tm, (lambda i: (i * tm) // seq_len)


def _silu(x):
    return x * jax.nn.sigmoid(x)


def _softplus(x):
    return jnp.maximum(x, 0.0) + jnp.log1p(jnp.exp(-jnp.abs(x)))


def _split3(x):
    x1 = x.astype(BF16)
    r = x - x1.astype(F32)
    x2 = r.astype(BF16)
    x3 = (r - x2.astype(F32)).astype(BF16)
    return x1, x2, x3


def _dot(a, b):
    return jnp.dot(a, b, preferred_element_type=F32)


def _norm_mod_rows(x_ref, nw_ref, shift, scale, h_ref, copy_ref=None):
    gain = nw_ref[...] * (1.0 + scale)
    tm = x_ref.shape[0]
    rows = min(NORM_ROWS, tm)
    blocks = []
    for r in range(0, tm, rows):
        x = x_ref[r:r + rows, :]
        if copy_ref is not None:
            copy_ref[r:r + rows, :] = x
        ms = jnp.mean(x * x, axis=-1, keepdims=True)
        hb = (x * lax.rsqrt(ms + EPS) * gain + shift).astype(h_ref.dtype)
        h_ref[r:r + rows, :] = hb
        blocks.append(hb)
    return jnp.concatenate(blocks, axis=0)


def _mod_kernel(c_ref, w_ref, b_ref, o_ref):
    s = _silu(c_ref[...])
    w = w_ref[...]
    s1 = s.astype(BF16)
    s2 = (s - s1.astype(F32)).astype(BF16)
    w1 = w.astype(BF16)
    w2 = (w - w1.astype(F32)).astype(BF16)
    o_ref[...] = (_dot(s1, w1) + _dot(s2, w1) + _dot(s1, w2)) + b_ref[...]


def _mod_call(cond, w_mod, b_mod):
    depth, d, n = w_mod.shape
    rows = cond.shape[0]
    tn = _tile(n, 1024)
    return pl.pallas_call(
        _mod_kernel,
        grid=(depth, n // tn),
        in_specs=[
            pl.BlockSpec((rows, d), lambda l, j: (0, 0)),
            pl.BlockSpec((None, d, tn), lambda l, j: (l, 0, j)),
            pl.BlockSpec((None, 1, tn), lambda l, j: (l, 0, j)),
        ],
        out_specs=pl.BlockSpec((None, rows, tn), lambda l, j: (l, 0, j)),
        out_shape=jax.ShapeDtypeStruct((depth, rows, n), F32),
        compiler_params=_params("parallel", "parallel"),
        name="mod",
    )(cond, w_mod, b_mod.reshape(depth, 1, n))


def _in_kernel(x_ref, mod_ref, nw_ref, w_ref, *rest, with_dt):
    if with_dt:
        wdtT_ref, o_ref, dtT_ref, h_ref = rest
    else:
        o_ref, h_ref = rest

    j = pl.program_id(1)

    @pl.when(j == 0)
    def _():
        h = _norm_mod_rows(x_ref, nw_ref, mod_ref[0:1, :], mod_ref[1:2, :], h_ref)
        o_ref[...] = _dot(h, w_ref[...]).astype(o_ref.dtype)
        if with_dt:
            dtT_ref[...] = lax.dot_general(wdtT_ref[...], h, (((1,), (1,)), ((), ())),
                                           preferred_element_type=F32)

    @pl.when(j > 0)
    def _():
        o_ref[...] = _dot(h_ref[...], w_ref[...]).astype(o_ref.dtype)


def _in_call(x, mods, norm_w, w, layer, wl, n, seq_len, tn_pref, wdtT=None):
    t, d = x.shape
    tm, cid = _row_tile(t, seq_len, mods.shape[1], 1024)
    tn = _tile(n, tn_pref)
    with_dt = wdtT is not None
    in_specs = [
        pl.BlockSpec((tm, d), lambda i, j: (i, 0)),
        pl.BlockSpec((None, None, N_MOD, d), lambda i, j: (layer, cid(i), 0, 0)),
        pl.BlockSpec((None, 1, d), lambda i, j: (layer, 0, 0)),
        pl.BlockSpec((None, d, tn), lambda i, j: (wl, 0, j)),
    ]
    out_specs = [pl.BlockSpec((tm, tn), lambda i, j: (i, j))]
    out_shape = [jax.ShapeDtypeStruct((t, n), BF16)]
    args = [x, mods, norm_w, w]
    if with_dt:
        h2 = wdtT.shape[1]
        in_specs.append(pl.BlockSpec((None, h2, d), lambda i, j: (wl, 0, 0)))
        out_specs.append(pl.BlockSpec((h2, tm), lambda i, j: (0, i)))
        out_shape.append(jax.ShapeDtypeStruct((h2, t), F32))
        args.append(wdtT)
    return pl.pallas_call(
        functools.partial(_in_kernel, with_dt=with_dt),
        grid=(t // tm, n // tn),
        in_specs=in_specs,
        out_specs=out_specs,
        out_shape=out_shape,
        scratch_shapes=[pltpu.VMEM((tm, d), BF16)],
        compiler_params=_params("parallel", "arbitrary"),
        name="in_proj",
    )(*args)


def _gelu_tanh(x):
    k = math.sqrt(2.0 / math.pi)
    a = -2.0 * k * 0.044715 * LOG2E
    b = -2.0 * k * LOG2E
    return x / (1.0 + jnp.exp2(x * (x * x * a + b)))


def _conv_kernel(cur_ref, prev_ref, next_ref, w_ref, b_ref, o_ref, *, nc):
    c = pl.program_id(1)
    bq, cc = cur_ref.shape
    q = min(bq, SSD_CHUNK)
    rows = q + 2 * HALO
    ri = lax.broadcasted_iota(I32, (q, rows), 0)
    ci = lax.broadcasted_iota(I32, (q, rows), 1)
    picks = [None if k == D_CONV // 2 else
             jnp.where(ci == ri + (HALO + k - D_CONV // 2), 1.0, 0.0).astype(BF16) for k in range(D_CONV)]
    cw = min(CONV_COLS, cc)
    nsub = bq // q
    for n0 in range(0, cc, cw):
        cs = slice(n0, n0 + cw)
        prev = jnp.where(c > 0, prev_ref[:, cs], jnp.zeros((HALO, cw), BF16))
        nxt = jnp.where(c < nc - 1, next_ref[:, cs], jnp.zeros((HALO, cw), BF16))
        for u in range(nsub):
            cur = cur_ref[u * q:(u + 1) * q, cs]
            before = prev if u == 0 else cur_ref[u * q - HALO:u * q, cs]
            after = nxt if u == nsub - 1 else cur_ref[(u + 1) * q:(u + 1) * q + HALO, cs]
            ext = jnp.concatenate([before, cur, after], axis=0)
            acc = b_ref[:, cs]
            for k in range(D_CONV):
                tap = cur.astype(F32) if picks[k] is None else _dot(picks[k], ext)
                acc = acc + w_ref[k:k + 1, cs] * tap
            o_ref[u * q:(u + 1) * q, cs] = _silu(acc).astype(o_ref.dtype)


def _conv_call(proj, conv_w, conv_b, layer, seq_len, col_block):
    t = proj.shape[0]
    cc = conv_w.shape[2]
    q = _tile(seq_len, 2 * SSD_CHUNK)
    nc = seq_len // q
    nseq = t // seq_len
    per = q // HALO
    last = t // HALO - 1
    return pl.pallas_call(
        functools.partial(_conv_kernel, nc=nc),
        grid=(nseq, nc),
        in_specs=[
            pl.BlockSpec((q, cc), lambda s, c: (s * nc + c, col_block)),
            pl.BlockSpec((HALO, cc), lambda s, c: (jnp.maximum((s * nc + c) * per - 1, 0), col_block)),
            pl.BlockSpec((HALO, cc), lambda s, c: (jnp.minimum((s * nc + c + 1) * per, last), col_block)),
            pl.BlockSpec((None, D_CONV, cc), lambda s, c: (layer, 0, 0)),
            pl.BlockSpec((None, 1, cc), lambda s, c: (layer, 0, 0)),
        ],
        out_specs=pl.BlockSpec((q, cc), lambda s, c: (s * nc + c, 0)),
        out_shape=jax.ShapeDtypeStruct((t, cc), BF16),
        compiler_params=_params("parallel", "arbitrary"),
        name="conv",
    )(proj, proj, proj, conv_w, conv_b)


def _ssd_factors(dtT_ref, lanes, dtbT_ref, alogT_ref, rev, heads):
    q = lanes.stop - lanes.start

    def dsel(fwd, bwd):
        return jnp.where(rev, bwd, fwd)

    dtT = _softplus(dsel(dtT_ref[0:heads, lanes], dtT_ref[heads:2 * heads, lanes])
                    + dsel(dtbT_ref[0:heads, :], dtbT_ref[heads:2 * heads, :]))
    aT = dtT * (-jnp.exp(dsel(alogT_ref[0:heads, :], alogT_ref[heads:2 * heads, :]))) * LOG2E
    ri = lax.broadcasted_iota(I32, (q, q), 0)
    ci = lax.broadcasted_iota(I32, (q, q), 1)
    triT = dsel(jnp.where(ci >= ri, 1.0, 0.0), jnp.where(ri >= ci, 1.0, 0.0)).astype(BF16)
    b1, b2, b3 = _split3(aT)
    a_csT = (_dot(b1, triT) + _dot(b2, triT)) + _dot(b3, triT)
    a_cs = a_csT.T
    tot = dsel(a_cs[q - 1:q, :], a_cs[0:1, :])
    stack = jnp.concatenate([dtT.T, jnp.exp2(a_cs), jnp.exp2(tot - a_cs)], axis=0)
    s_hi = stack.astype(BF16)
    s_lo = (stack - s_hi.astype(F32)).astype(BF16)
    return a_cs, a_csT, jnp.concatenate([s_hi, s_lo], axis=1)


def _ssd_kernel(xbc_ref, dtT_ref, dtTn_ref, dtbT_ref, alogT_ref, dsk_ref,
                *rest, nc, heads, hdim, groups, nstate, has_h0, want_state):
    rest = list(rest)
    h0_ref = so_ref = None
    if has_h0:
        h0_ref = rest.pop(0)
    y_ref = rest.pop(0)
    if want_state:
        so_ref = rest.pop(0)
    st_ref, yf_ref, acs_ref, acsT_ref, scat_ref = rest

    q = SSD_CHUNK
    bq = xbc_ref.shape[0]
    nsub = bq // q
    db = heads * hdim
    gn = groups * nstate
    hpg = heads // groups
    gw = hpg * hdim
    seq_len = nc * bq
    t = pl.program_id(1)
    rev = t >= nc
    c = jnp.where(rev, 2 * nc - 1 - t, t)
    slot = t & 1

    def dsel(fwd, bwd):
        return jnp.where(rev, bwd, fwd)

    def stage(src_ref, is_rev, dst):
        for s in range(nsub):
            a_cs, a_csT, s_cat = _ssd_factors(src_ref, slice(s * q, (s + 1) * q), dtbT_ref, alogT_ref, is_rev,
                                              heads)
            acs_ref[dst, s] = a_cs
            acsT_ref[dst, s] = a_csT
            scat_ref[dst, s] = s_cat

    @pl.when(t == 0)
    def _():
        stage(dtT_ref, rev, slot)

    @pl.when((t == 0) | (t == nc))
    def _():
        st_ref[...] = h0_ref[...] if has_h0 else jnp.zeros_like(st_ref)

    ri = lax.broadcasted_iota(I32, (q, q), 0)
    ci = lax.broadcasted_iota(I32, (q, q), 1)
    mask = dsel(jnp.where(ri >= ci, 1.0, 0.0), jnp.where(ci >= ri, 1.0, 0.0)) > 0.5

    assert hdim & (hdim - 1) == 0 and heads & (heads - 1) == 0
    eh = lax.broadcasted_iota(I32, (2 * heads, db), 0) & (heads - 1)
    ec = lax.shift_right_logical(lax.broadcasted_iota(I32, (2 * heads, db), 1), hdim.bit_length() - 1)
    expand = jnp.where(eh == ec, 1.0, 0.0).astype(BF16)
    low_half = (lax.broadcasted_iota(I32, (q, gw), 1) & hdim) == 0

    for u in range(nsub):
        su = jnp.where(rev, nsub - 1 - u, u) if nsub > 1 else 0
        rows = pl.ds(pl.multiple_of(su * q, q), q) if nsub > 1 else slice(None)
        row0 = pl.multiple_of(c * bq + su * q, q)
        dst0 = pl.multiple_of(jnp.where(rev, seq_len, row0), q)
        a_cs = acs_ref[slot, su]
        a_csT = acsT_ref[slot, su]
        s_cat = scat_ref[slot, su]

        for g in range(groups):
            gc = slice(g * gw, (g + 1) * gw)
            ex = _dot(s_cat, expand[:, gc])
            dt_x, e_x, dec_x = ex[0:q], ex[q:2 * q], ex[2 * q:3 * q]
            xg = xbc_ref[rows, gc].astype(F32) * dt_x
            xg_lo = jnp.where(low_half, xg, 0.0).astype(BF16)
            xg_hi = jnp.where(low_half, 0.0, xg).astype(BF16)
            xdb = (xg * dec_x).astype(BF16)
            bg = xbc_ref[rows, db + g * nstate:db + (g + 1) * nstate]
            cg = xbc_ref[rows, db + gn + g * nstate:db + gn + (g + 1) * nstate]
            cb = lax.dot_general(cg, bg, (((1,), (1,)), ((), ())), preferred_element_type=F32)
            yo = _dot(cg, st_ref[:, gc].astype(BF16)) * e_x
            for pr in range(hpg // 2):
                h0 = g * hpg + 2 * pr
                ls = []
                for h in (h0, h0 + 1):
                    d = a_cs[:, h:h + 1] - a_csT[h:h + 1, :]
                    ls.append(jnp.where(mask, jnp.exp2(d), 0.0) * cb)
                lhs = jnp.concatenate(ls, axis=1).astype(BF16)
                pc = slice(2 * pr * hdim, (2 * pr + 2) * hdim)
                rhs = jnp.concatenate([xg_lo[:, pc], xg_hi[:, pc]], axis=0)
                yp = _dot(lhs, rhs) + yo[:, pc]
                cols = slice(h0 * hdim, (h0 + 2) * hdim)
                yf_ref[pl.ds(dst0, q), cols] = yp.astype(yf_ref.dtype)
                y_ref[rows, cols] = ((yf_ref[pl.ds(row0, q), cols].astype(F32) + yp)
                                     + dsk_ref[:, cols] * xbc_ref[rows, cols].astype(F32)).astype(y_ref.dtype)
            new_state = lax.dot_general(bg, xdb, (((0,), (0,)), ((), ())), preferred_element_type=F32)
            st_ref[:, gc] = st_ref[:, gc] * dsel(e_x[q - 1:q, :], e_x[0:1, :]) + new_state

    stage(dtTn_ref, t + 1 >= nc, 1 - slot)

    if want_state:
        @pl.when((t == nc - 1) | (t == 2 * nc - 1))
        def _():
            so_ref[...] = st_ref[...].T


def _ssd_call(xbc, dtT, dt_biasT, a_logT, d_skip_x, layer, seq_len, heads, hdim, nstate, h0T=None, state_out=None):
    t, cc = xbc.shape
    q = SSD_CHUNK
    bq = _tile(seq_len, SSD_STEP_CHUNKS * q)
    nsub = bq // q
    nc = seq_len // bq
    nseq = t // seq_len
    db = heads * hdim
    h2 = 2 * heads
    has_h0 = h0T is not None
    want_state = state_out is not None
    n_even = dt_biasT.shape[0]

    def chunk(tt):
        return jnp.where(tt >= nc, 2 * nc - 1 - tt, tt)

    def late(tt):
        return jnp.where(tt >= nc, 2 * nc - 1 - tt, nc - 1)

    def direction(tt):
        return jnp.where(tt >= nc, 1, 0)

    in_specs = [
        pl.BlockSpec((bq, cc), lambda s, tt: (s * nc + chunk(tt), 0)),
        pl.BlockSpec((h2, bq), lambda s, tt: (0, s * nc + chunk(tt))),
        pl.BlockSpec((h2, bq), lambda s, tt: (0, s * nc + chunk(jnp.minimum(tt + 1, 2 * nc - 1)))),
        pl.BlockSpec((None, h2, 1), lambda s, tt: (layer, 0, 0)),
        pl.BlockSpec((None, h2, 1), lambda s, tt: (layer, 0, 0)),
        pl.BlockSpec((None, 1, db), lambda s, tt: (layer, 0, 0)),
    ]
    args = [xbc, dtT, dtT, dt_biasT, a_logT, d_skip_x]
    if has_h0:
        in_specs.append(pl.BlockSpec((None, None, None, nstate, db),
                                     lambda s, tt: (s, layer, direction(tt), 0, 0)))
        args.append(h0T)
    out_specs = [pl.BlockSpec((bq, db), lambda s, tt: (s * nc + late(tt), 0))]
    out_shape = [jax.ShapeDtypeStruct((t, db), BF16)]
    aliases = {}
    if want_state:
        out_specs.append(pl.BlockSpec((None, None, None, db, nstate),
                                      lambda s, tt: (s, layer, direction(tt), 0, 0)))
        out_shape.append(jax.ShapeDtypeStruct((nseq, n_even, 2, db, nstate), F32))
        if not isinstance(state_out, str):
            in_specs.append(pl.BlockSpec(memory_space=pl.ANY))
            args.append(state_out)
            aliases = {len(args) - 1: 1}
    kern = functools.partial(_ssd_kernel, nc=nc, heads=heads, hdim=hdim, groups=SSD_GROUPS, nstate=nstate,
                             has_h0=has_h0, want_state=want_state)
    if aliases:
        inner = kern
        n_in = len(args)

        def kern(*refs):
            return inner(*refs[:n_in - 1], *refs[n_in:])

    return pl.pallas_call(
        kern,
        grid=(nseq, 2 * nc),
        in_specs=in_specs,
        out_specs=out_specs,
        out_shape=out_shape,
        input_output_aliases=aliases,
        scratch_shapes=[
            pltpu.VMEM((nstate, db), F32),
            pltpu.VMEM((seq_len + q, db), BF16),
            pltpu.VMEM((2, nsub, q, heads), F32),
            pltpu.VMEM((2, nsub, heads, q), F32),
            pltpu.VMEM((2, nsub, 3 * q, h2), BF16),
        ],
        compiler_params=_params("parallel", "arbitrary"),
        name="ssd",
    )(*args)


def _mix_even_kernel(u_ref, v_ref, z_ref, ys_ref, ws_ref, bsT_ref, vn_ref, nw_ref, w_ref, x_ref, mod_ref, o_ref,
                     b_ref, *, groups):
    tm, d_a = u_ref.shape
    d_b = ys_ref.shape[1]
    ch = d_a // groups
    assert d_a == d_b
    gz = ys_ref[...].astype(F32) * _silu(z_ref[...].astype(F32))
    ms = jnp.mean(gz * gz, axis=-1, keepdims=True)
    b_ref[...] = (gz * lax.rsqrt(ms + EPS) * nw_ref[...]).astype(BF16)
    acc = None
    for g in range(groups):
        sl = slice(g * ch, (g + 1) * ch)
        part = _dot(b_ref[:, sl], w_ref[d_a + g * ch:d_a + (g + 1) * ch, :])
        acc = part if acc is None else acc + part
        vns = []
        for c0 in range(0, tm, CHUNK):
            v = _gelu_tanh(v_ref[c0:c0 + CHUNK, sl].astype(F32))
            ms = jnp.mean(v * v, axis=-1, keepdims=True)
            vns.append((v * lax.rsqrt(ms + EPS) * vn_ref[:, sl]).astype(BF16))
        sv = _dot(ws_ref[g], jnp.concatenate(vns, axis=1)) + bsT_ref[:, g:g + 1]
        parts = [(_gelu_tanh(u_ref[c * CHUNK:(c + 1) * CHUNK, sl].astype(F32)) * sv[:, c * ch:(c + 1) * ch])
                 .astype(BF16) for c in range(tm // CHUNK)]
        acc = acc + _dot(jnp.concatenate(parts, axis=0), w_ref[sl, :])
    o_ref[...] = x_ref[...] + mod_ref[2:3, :] * acc


def _mix_even_call(proj, ysum, z_block, ws, bsT, vnorm, norm_w, w, x, mods, layer, wl, seq_len):
    t, d = x.shape
    d_b = ysum.shape[1]
    d_a = w.shape[1] - d_b
    groups = ws.shape[1]
    assert d_a == d_b
    tm, cid = _row_tile(t, seq_len, mods.shape[1], 2 * CHUNK)
    return pl.pallas_call(
        functools.partial(_mix_even_kernel, groups=groups),
        grid=(t // tm,),
        in_specs=[
            pl.BlockSpec((tm, d_a), lambda i: (i, 0)),
            pl.BlockSpec((tm, d_a), lambda i: (i, 1)),
            pl.BlockSpec((tm, d_b), lambda i: (i, z_block)),
            pl.BlockSpec((tm, d_b), lambda i: (i, 0)),
            pl.BlockSpec((None, groups, CHUNK, CHUNK), lambda i: (wl, 0, 0, 0)),
            pl.BlockSpec((None, CHUNK, groups), lambda i: (wl, 0, 0)),
            pl.BlockSpec((None, 1, d_a), lambda i: (wl, 0, 0)),
            pl.BlockSpec((None, 1, d_b), lambda i: (wl, 0, 0)),
            pl.BlockSpec((None, d_a + d_b, d), lambda i: (wl, 0, 0), pipeline_mode=pl.Buffered(1)),
            pl.BlockSpec((tm, d), lambda i: (i, 0)),
            pl.BlockSpec((None, None, N_MOD, d), lambda i: (layer, cid(i), 0, 0)),
        ],
        out_specs=pl.BlockSpec((tm, d), lambda i: (i, 0)),
        out_shape=jax.ShapeDtypeStruct((t, d), F32),
        scratch_shapes=[pltpu.VMEM((tm, d_b), BF16)],
        compiler_params=_params("parallel"),
        name="mix_even",
    )(proj, proj, proj, ysum, ws, bsT, vnorm, norm_w, w, x, mods)


def _pool_kernel(hc_ref, pw_ref, ps_ref, wo_ref, x_ref, mod_ref, o_ref, *, win, sub):
    tm, d_c = hc_ref.shape
    ng = len(POOL_WINDOWS)
    pch = d_c // ng
    assert win & (win - 1) == 0
    ri = lax.broadcasted_iota(I32, (sub, sub), 0)
    ci = lax.broadcasted_iota(I32, (sub, sub), 1)
    same = lax.shift_right_logical(ri, win.bit_length() - 1) == lax.shift_right_logical(ci, win.bit_length() - 1)
    pos = lax.broadcasted_iota(I32, (sub, 1), 0) & (win - 1)
    mixed = []
    for g, k in enumerate(POOL_WINDOWS):
        off = ci - ri
        band = jnp.where(same & (off >= -(k // 2)) & (off < k - k // 2), 1.0, 0.0).astype(BF16)
        lo = jnp.maximum(pos - k // 2, 0)
        hi = jnp.minimum(pos - k // 2 + k, win)
        cnt = (hi - lo).astype(F32)
        pooled = []
        for r in range(tm // sub):
            xb = hc_ref[r * sub:(r + 1) * sub, g * pch:(g + 1) * pch]
            pooled.append(_dot(band, xb) / cnt - xb.astype(F32))
        pg = jnp.concatenate(pooled, axis=0).astype(BF16)
        mixed.append(_dot(pg, pw_ref[g]) * ps_ref[:, g * pch:(g + 1) * pch])
    mix = jnp.concatenate(mixed, axis=1).astype(BF16)
    o_ref[...] = x_ref[...] + mod_ref[2:3, :] * _dot(mix, wo_ref[...])


def _pool_call(hc, pool_w, pool_scale, w_out, x, mods, layer, wl, seq_len, win):
    t, d = x.shape
    d_c = hc.shape[1]
    _, ng, pch, _ = pool_w.shape
    sub = max(win, 128)
    assert sub % win == 0 and seq_len % sub == 0
    tm, cid = _row_tile(t, seq_len, mods.shape[1], 512)
    assert tm % sub == 0
    return pl.pallas_call(
        functools.partial(_pool_kernel, win=win, sub=sub),
        grid=(t // tm,),
        in_specs=[
            pl.BlockSpec((tm, d_c), lambda i: (i, 0)),
            pl.BlockSpec((None, ng, pch, pch), lambda i: (wl, 0, 0, 0)),
            pl.BlockSpec((None, 1, d_c), lambda i: (wl, 0, 0)),
            pl.BlockSpec((None, d_c, d), lambda i: (wl, 0, 0)),
            pl.BlockSpec((tm, d), lambda i: (i, 0)),
            pl.BlockSpec((None, None, N_MOD, d), lambda i: (layer, cid(i), 0, 0)),
        ],
        out_specs=pl.BlockSpec((tm, d), lambda i: (i, 0)),
        out_shape=jax.ShapeDtypeStruct((t, d), F32),
        compiler_params=_params("parallel"),
        name="pool_out",
    )(hc, pool_w, pool_scale, w_out, x, mods)


def _ffn_kernel(x_ref, mod_ref, nw_ref, w1_ref, w3_ref, w2_ref, fn_ref, o_ref, h_ref, *, nf, final, dn):
    f = pl.program_id(1)
    d = o_ref.shape[1]

    def step(h):
        act = (_silu(_dot(h, w1_ref[...])) * _dot(h, w3_ref[...])).astype(BF16)
        for n0 in range(0, d, dn):
            o_ref[:, n0:n0 + dn] += mod_ref[5:6, n0:n0 + dn] * _dot(act, w2_ref[:, n0:n0 + dn])

    @pl.when(f == 0)
    def _():
        step(_norm_mod_rows(x_ref, nw_ref, mod_ref[3:4, :], mod_ref[4:5, :], h_ref, copy_ref=o_ref))

    @pl.when(f > 0)
    def _():
        step(h_ref[...])

    if final:
        @pl.when(f == nf - 1)
        def _():
            rows = min(NORM_ROWS, o_ref.shape[0])

            def body(r, carry):
                sl = pl.ds(pl.multiple_of(r * rows, rows), rows)
                y = o_ref[sl, :]
                ms = jnp.mean(y * y, axis=-1, keepdims=True)
                o_ref[sl, :] = y * lax.rsqrt(ms + EPS) * fn_ref[...]
                return carry

            lax.fori_loop(0, o_ref.shape[0] // rows, body, 0)


def _ffn_call(x, mods, norm_w, w1, w3, w2, final_nw, layer, seq_len, final):
    t, d = x.shape
    dff = w1.shape[2]
    tm, cid = _row_tile(t, seq_len, mods.shape[1], 1024)
    tf = _tile(dff, 512)
    nf = dff // tf
    x_spec = pl.BlockSpec((tm, d), lambda i, f: (i, 0))
    return pl.pallas_call(
        functools.partial(_ffn_kernel, nf=nf, final=final, dn=_tile(d, 512)),
        grid=(t // tm, nf),
        in_specs=[
            x_spec,
            pl.BlockSpec((None, None, N_MOD, d), lambda i, f: (layer, cid(i), 0, 0)),
            pl.BlockSpec((None, 1, d), lambda i, f: (layer, 0, 0)),
            pl.BlockSpec((None, d, tf), lambda i, f: (layer, 0, f)),
            pl.BlockSpec((None, d, tf), lambda i, f: (layer, 0, f)),
            pl.BlockSpec((None, tf, d), lambda i, f: (layer, f, 0)),
            pl.BlockSpec((1, d), lambda i, f: (0, 0)),
        ],
        out_specs=pl.BlockSpec((tm, d), lambda i, f: (i, 0)),
        out_shape=jax.ShapeDtypeStruct((t, d), F32),
        scratch_shapes=[pltpu.VMEM((tm, d), BF16)],
        compiler_params=_params("parallel", "arbitrary"),
        name="ffn",
    )(x, mods, norm_w, w1, w3, w2, final_nw)


def _run_group(x, seq_len, mods, win, h0T, want_state, p):
    depth = mods.shape[0]
    state = "new" if want_state else None
    for i in range(depth):
        if i % 2 == 0:
            e = i // 2
            proj, dtT = _in_call(x, mods, p["norm_mix"], p["w_in_even"], i, e, p["n_main"], seq_len, 2304,
                                 p["w_in_dtT"])
            xbc = _conv_call(proj, p["ssd_conv_w"], p["ssd_conv_b"], e, seq_len, p["xbc_block"])
            res = _ssd_call(xbc, dtT, p["ssd_dt_biasT"], p["ssd_a_logT"], p["ssd_d_x"], e, seq_len,
                            p["heads"], p["hdim"], p["nstate"], h0T, state)
            if want_state:
                state = res[1]
            x = _mix_even_call(proj, res[0], p["z_block"], p["gmlp_ws"], p["gmlp_bsT"], p["gmlp_norm"],
                               p["ssd_norm"], p["w_out_even"], x, mods, i, e, seq_len)
        else:
            o = i // 2
            (hc,) = _in_call(x, mods, p["norm_mix"], p["w_in_odd"], i, o, p["w_in_odd"].shape[2], seq_len, 2048)
            x = _pool_call(hc, p["pool_w"], p["pool_scale"], p["w_out_odd"], x, mods, i, o, seq_len, win)
        x = _ffn_call(x, mods, p["norm_ffn"], p["ffn_w1"], p["ffn_w3"], p["ffn_w2"], p["final_norm"], i, seq_len,
                      final=(i == depth - 1))
    return x, state


def kernel(x_prompt, x_sample, state_ssd, c, c_ctx, w_mod, b_mod, norm_mix, norm_ffn, w_in_even, w_out_even, gmlp_norm, gmlp_ws, gmlp_bs, ssd_conv_w, ssd_conv_b, ssd_dt_bias, ssd_a_log, ssd_d, ssd_norm, w_in_odd, pool_w, pool_scale, w_out_odd, ffn_w1, ffn_w3, ffn_w2, final_norm):
    batch, seq, d = x_prompt.shape
    dec_batch, dec_seq, _ = x_sample.shape
    depth = w_mod.shape[0]
    n_even = w_in_even.shape[0]
    heads, hdim, nstate = state_ssd.shape[3:]
    d_b = heads * hdim
    d_a = gmlp_norm.shape[1]
    cc = ssd_conv_w.shape[2]
    n_main = 2 * d_a + d_b + cc
    h2 = 2 * heads
    assert w_in_even.shape[2] == n_main + h2 and h2 <= LANES
    assert (2 * d_a) % d_b == 0 and (n_main - cc) % cc == 0

    n_cond = 1 + dec_batch
    rows = -(-n_cond // 8) * 8
    cond = jnp.concatenate([c_ctx[None], c, jnp.zeros((rows - n_cond, d), F32)], axis=0)
    mods = _mod_call(cond, w_mod, b_mod).reshape(depth, rows, N_MOD, d)

    p = dict(
        heads=heads, hdim=hdim, nstate=nstate, n_main=n_main,
        z_block=(2 * d_a) // d_b, xbc_block=(n_main - cc) // cc,
        norm_mix=norm_mix.reshape(depth, 1, d), norm_ffn=norm_ffn.reshape(depth, 1, d),
        w_in_even=w_in_even.astype(BF16),
        w_in_dtT=jnp.swapaxes(w_in_even[:, :, n_main:], 1, 2).astype(BF16),
        w_out_even=w_out_even.astype(BF16),
        gmlp_norm=gmlp_norm.reshape(n_even, 1, d_a), gmlp_ws=gmlp_ws.astype(BF16),
        gmlp_bsT=jnp.swapaxes(gmlp_bs, 1, 2),
        ssd_conv_w=ssd_conv_w, ssd_conv_b=ssd_conv_b.reshape(n_even, 1, cc),
        ssd_dt_biasT=ssd_dt_bias.reshape(n_even, h2, 1), ssd_a_logT=ssd_a_log.reshape(n_even, h2, 1),
        ssd_d_x=jnp.repeat(ssd_d, hdim, axis=1).reshape(n_even, 1, d_b),
        ssd_norm=ssd_norm.reshape(n_even, 1, d_b),
        w_in_odd=w_in_odd.astype(BF16), pool_w=pool_w.astype(BF16),
        pool_scale=pool_scale.reshape(-1, 1, d), w_out_odd=w_out_odd.astype(BF16),
        ffn_w1=ffn_w1.astype(BF16), ffn_w3=ffn_w3.astype(BF16), ffn_w2=ffn_w2.astype(BF16),
        final_norm=final_norm.reshape(1, d),
    )

    y_p, state_new = _run_group(x_prompt.reshape(batch * seq, d), seq, mods[:, 0:1], seq, None, True, p)
    h0T = jnp.swapaxes(state_ssd.reshape(dec_batch, n_even, 2, d_b, nstate), 3, 4)
    y_s, _ = _run_group(x_sample.reshape(dec_batch * dec_seq, d), dec_seq, mods[:, 1:n_cond], GRID_W, h0T, False, p)

    return (y_p.reshape(batch, seq, d), y_s.reshape(dec_batch, dec_seq, d),
            state_new.reshape(batch, n_even, 2, heads, hdim, nstate))
```

```python
import functools
import math

import jax
import jax.numpy as jnp
from jax import lax
from jax.experimental import pallas as pl
from jax.experimental.pallas import tpu as pltpu

F32 = jnp.float32
BF16 = jnp.bfloat16
I32 = jnp.int32

EPS = 1e-6
N_MOD = 6
GRID_W = 64
CHUNK = 128
SSD_CHUNK = 128
SSD_STEP_CHUNKS = 4
SSD_GROUPS = 4
D_CONV = 5
POOL_WINDOWS = (2, 4, 8, 16)
LANES = 128
HALO = 16
NORM_ROWS = 32
FINAL_NORM_ROWS = 256
CONV_COLS = 512
LOG2E = 1.4426950408889634
VMEM_LIMIT_BYTES = 56 * 1024 * 1024


def _params(*sem):
    return pltpu.CompilerParams(dimension_semantics=sem, vmem_limit_bytes=VMEM_LIMIT_BYTES)


def _tile(n, pref):
    t = min(n, pref)
    assert n % t == 0, (n, pref)
    return t


def _row_tile(t, seq_len, n_cond, pref):
    if n_cond == 1:
        return _tile(t, pref), (lambda i: 0)
    tm = _tile(seq_len, min(pref, seq_len))
    return tm, (lambda i: (i * tm) // seq_len)


def _silu(x):
    return x * jax.nn.sigmoid(x)


def _softplus(x):
    return jnp.maximum(x, 0.0) + jnp.log1p(jnp.exp(-jnp.abs(x)))


def _split3(x):
    x1 = x.astype(BF16)
    r = x - x1.astype(F32)
    x2 = r.astype(BF16)
    x3 = (r - x2.astype(F32)).astype(BF16)
    return x1, x2, x3


def _dot(a, b):
    return jnp.dot(a, b, preferred_element_type=F32)


def _norm_mod_rows(x_ref, nw_ref, shift, scale, h_ref, copy_ref=None):
    gain = nw_ref[...] * (1.0 + scale)
    tm = x_ref.shape[0]
    rows = min(NORM_ROWS, tm)
    blocks = []
    for r in range(0, tm, rows):
        x = x_ref[r:r + rows, :]
        if copy_ref is not None:
            copy_ref[r:r + rows, :] = x
        ms = jnp.mean(x * x, axis=-1, keepdims=True)
        hb = (x * lax.rsqrt(ms + EPS) * gain + shift).astype(h_ref.dtype)
        h_ref[r:r + rows, :] = hb
        blocks.append(hb)
    return jnp.concatenate(blocks, axis=0)


def _mod_kernel(c_ref, w_ref, b_ref, o_ref):
    s = _silu(c_ref[...])
    w = w_ref[...]
    s1 = s.astype(BF16)
    s2 = (s - s1.astype(F32)).astype(BF16)
    w1 = w.astype(BF16)
    w2 = (w - w1.astype(F32)).astype(BF16)
    o_ref[...] = (_dot(s1, w1) + _dot(s2, w1) + _dot(s1, w2)) + b_ref[...]


def _mod_call(cond, w_mod, b_mod):
    depth, d, n = w_mod.shape
    rows = cond.shape[0]
    tn = _tile(n, 1024)
    return pl.pallas_call(
        _mod_kernel,
        grid=(depth, n // tn),
        in_specs=[
            pl.BlockSpec((rows, d), lambda l, j: (0, 0)),
            pl.BlockSpec((None, d, tn), lambda l, j: (l, 0, j)),
            pl.BlockSpec((None, 1, tn), lambda l, j: (l, 0, j)),
        ],
        out_specs=pl.BlockSpec((None, rows, tn), lambda l, j: (l, 0, j)),
        out_shape=jax.ShapeDtypeStruct((depth, rows, n), F32),
        compiler_params=_params("parallel", "parallel"),
        name="mod",
    )(cond, w_mod, b_mod.reshape(depth, 1, n))


def _in_kernel(x_ref, mod_ref, nw_ref, w_ref, *rest, with_dt):
    if with_dt:
        wdtT_ref, o_ref, dtT_ref, h_ref = rest
    else:
        o_ref, h_ref = rest

    j = pl.program_id(1)

    @pl.when(j == 0)
    def _():
        h = _norm_mod_rows(x_ref, nw_ref, mod_ref[0:1, :], mod_ref[1:2, :], h_ref)
        o_ref[...] = _dot(h, w_ref[...]).astype(o_ref.dtype)
        if with_dt:
            dtT_ref[...] = lax.dot_general(wdtT_ref[...], h, (((1,), (1,)), ((), ())),
                                           preferred_element_type=F32)

    @pl.when(j > 0)
    def _():
        o_ref[...] = _dot(h_ref[...], w_ref[...]).astype(o_ref.dtype)


def _in_call(x, mods, norm_w, w, layer, wl, n, seq_len, tn_pref, wdtT=None):
    t, d = x.shape
    tm, cid = _row_tile(t, seq_len, mods.shape[1], 1024)
    tn = _tile(n, tn_pref)
    with_dt = wdtT is not None
    in_specs = [
        pl.BlockSpec((tm, d), lambda i, j: (i, 0)),
        pl.BlockSpec((None, None, N_MOD, d), lambda i, j: (layer, cid(i), 0, 0)),
        pl.BlockSpec((None, 1, d), lambda i, j: (layer, 0, 0)),
        pl.BlockSpec((None, d, tn), lambda i, j: (wl, 0, j)),
    ]
    out_specs = [pl.BlockSpec((tm, tn), lambda i, j: (i, j))]
    out_shape = [jax.ShapeDtypeStruct((t, n), BF16)]
    args = [x, mods, norm_w, w]
    if with_dt:
        h2 = wdtT.shape[1]
        in_specs.append(pl.BlockSpec((None, h2, d), lambda i, j: (wl, 0, 0)))
        out_specs.append(pl.BlockSpec((h2, tm), lambda i, j: (0, i)))
        out_shape.append(jax.ShapeDtypeStruct((h2, t), F32))
        args.append(wdtT)
    return pl.pallas_call(
        functools.partial(_in_kernel, with_dt=with_dt),
        grid=(t // tm, n // tn),
        in_specs=in_specs,
        out_specs=out_specs,
        out_shape=out_shape,
        scratch_shapes=[pltpu.VMEM((tm, d), BF16)],
        compiler_params=_params("parallel", "arbitrary"),
        name="in_proj",
    )(*args)


def _gelu_tanh(x):
    k = math.sqrt(2.0 / math.pi)
    a = -2.0 * k * 0.044715 * LOG2E
    b = -2.0 * k * LOG2E
    return x / (1.0 + jnp.exp2(x * (x * x * a + b)))


def _conv_kernel(cur_ref, prev_ref, next_ref, w_ref, b_ref, o_ref, *, nc):
    c = pl.program_id(1)
    bq, cc = cur_ref.shape
    q = min(bq, SSD_CHUNK)
    rows = q + 2 * HALO
    ri = lax.broadcasted_iota(I32, (q, rows), 0)
    ci = lax.broadcasted_iota(I32, (q, rows), 1)
    picks = [None if k == D_CONV // 2 else
             jnp.where(ci == ri + (HALO + k - D_CONV // 2), 1.0, 0.0).astype(BF16) for k in range(D_CONV)]
    cw = min(CONV_COLS, cc)
    nsub = bq // q
    for n0 in range(0, cc, cw):
        cs = slice(n0, n0 + cw)
        prev = jnp.where(c > 0, prev_ref[:, cs], jnp.zeros((HALO, cw), BF16))
        nxt = jnp.where(c < nc - 1, next_ref[:, cs], jnp.zeros((HALO, cw), BF16))
        for u in range(nsub):
            cur = cur_ref[u * q:(u + 1) * q, cs]
            before = prev if u == 0 else cur_ref[u * q - HALO:u * q, cs]
            after = nxt if u == nsub - 1 else cur_ref[(u + 1) * q:(u + 1) * q + HALO, cs]
            ext = jnp.concatenate([before, cur, after], axis=0)
            acc = b_ref[:, cs]
            for k in range(D_CONV):
                tap = cur.astype(F32) if picks[k] is None else _dot(picks[k], ext)
                acc = acc + w_ref[k:k + 1, cs] * tap
            o_ref[u * q:(u + 1) * q, cs] = _silu(acc).astype(o_ref.dtype)


def _conv_call(proj, conv_w, conv_b, layer, seq_len, col_block):
    t = proj.shape[0]
    cc = conv_w.shape[2]
    q = _tile(seq_len, 2 * SSD_CHUNK)
    nc = seq_len // q
    nseq = t // seq_len
    per = q // HALO
    last = t // HALO - 1
    return pl.pallas_call(
        functools.partial(_conv_kernel, nc=nc),
        grid=(nseq, nc),
        in_specs=[
            pl.BlockSpec((q, cc), lambda s, c: (s * nc + c, col_block)),
            pl.BlockSpec((HALO, cc), lambda s, c: (jnp.maximum((s * nc + c) * per - 1, 0), col_block)),
            pl.BlockSpec((HALO, cc), lambda s, c: (jnp.minimum((s * nc + c + 1) * per, last), col_block)),
            pl.BlockSpec((None, D_CONV, cc), lambda s, c: (layer, 0, 0)),
            pl.BlockSpec((None, 1, cc), lambda s, c: (layer, 0, 0)),
        ],
        out_specs=pl.BlockSpec((q, cc), lambda s, c: (s * nc + c, 0)),
        out_shape=jax.ShapeDtypeStruct((t, cc), BF16),
        compiler_params=_params("parallel", "arbitrary"),
        name="conv",
    )(proj, proj, proj, conv_w, conv_b)


def _ssd_factors(dtT_ref, lanes, dtbT_ref, alogT_ref, rev, heads):
    q = lanes.stop - lanes.start

    def dsel(fwd, bwd):
        return jnp.where(rev, bwd, fwd)

    dtT = _softplus(dsel(dtT_ref[0:heads, lanes], dtT_ref[heads:2 * heads, lanes])
                    + dsel(dtbT_ref[0:heads, :], dtbT_ref[heads:2 * heads, :]))
    aT = dtT * (-jnp.exp(dsel(alogT_ref[0:heads, :], alogT_ref[heads:2 * heads, :]))) * LOG2E
    ri = lax.broadcasted_iota(I32, (q, q), 0)
    ci = lax.broadcasted_iota(I32, (q, q), 1)
    triT = dsel(jnp.where(ci >= ri, 1.0, 0.0), jnp.where(ri >= ci, 1.0, 0.0)).astype(BF16)
    b1, b2, b3 = _split3(aT)
    a_csT = (_dot(b1, triT) + _dot(b2, triT)) + _dot(b3, triT)
    a_cs = a_csT.T
    tot = dsel(a_cs[q - 1:q, :], a_cs[0:1, :])
    stack = jnp.concatenate([dtT.T, jnp.exp2(a_cs), jnp.exp2(tot - a_cs)], axis=0)
    s_hi = stack.astype(BF16)
    s_lo = (stack - s_hi.astype(F32)).astype(BF16)
    return a_cs, a_csT, jnp.concatenate([s_hi, s_lo], axis=1)


def _ssd_kernel(xbc_ref, dtT_ref, dtTn_ref, dtbT_ref, alogT_ref, dsk_ref,
                *rest, nc, heads, hdim, groups, nstate, has_h0, want_state):
    rest = list(rest)
    h0_ref = so_ref = None
    if has_h0:
        h0_ref = rest.pop(0)
    y_ref = rest.pop(0)
    if want_state:
        so_ref = rest.pop(0)
    st_ref, yf_ref, acs_ref, acsT_ref, scat_ref = rest

    q = SSD_CHUNK
    bq = xbc_ref.shape[0]
    nsub = bq // q
    db = heads * hdim
    gn = groups * nstate
    hpg = heads // groups
    gw = hpg * hdim
    seq_len = nc * bq
    t = pl.program_id(1)
    rev = t >= nc
    c = jnp.where(rev, 2 * nc - 1 - t, t)
    slot = t & 1

    def dsel(fwd, bwd):
        return jnp.where(rev, bwd, fwd)

    def stage(src_ref, is_rev, dst):
        for s in range(nsub):
            a_cs, a_csT, s_cat = _ssd_factors(src_ref, slice(s * q, (s + 1) * q), dtbT_ref, alogT_ref, is_rev,
                                              heads)
            acs_ref[dst, s] = a_cs
            acsT_ref[dst, s] = a_csT
            scat_ref[dst, s] = s_cat

    @pl.when(t == 0)
    def _():
        stage(dtT_ref, rev, slot)

    @pl.when((t == 0) | (t == nc))
    def _():
        st_ref[...] = h0_ref[...] if has_h0 else jnp.zeros_like(st_ref)

    ri = lax.broadcasted_iota(I32, (q, q), 0)
    ci = lax.broadcasted_iota(I32, (q, q), 1)
    mask = dsel(jnp.where(ri >= ci, 1.0, 0.0), jnp.where(ci >= ri, 1.0, 0.0)) > 0.5

    assert hdim & (hdim - 1) == 0 and heads & (heads - 1) == 0
    eh = lax.broadcasted_iota(I32, (2 * heads, db), 0) & (heads - 1)
    ec = lax.shift_right_logical(lax.broadcasted_iota(I32, (2 * heads, db), 1), hdim.bit_length() - 1)
    expand = jnp.where(eh == ec, 1.0, 0.0).astype(BF16)
    low_half = (lax.broadcasted_iota(I32, (q, gw), 1) & hdim) == 0

    for u in range(nsub):
        su = jnp.where(rev, nsub - 1 - u, u) if nsub > 1 else 0
        rows = pl.ds(pl.multiple_of(su * q, q), q) if nsub > 1 else slice(None)
        row0 = pl.multiple_of(c * bq + su * q, q)
        dst0 = pl.multiple_of(jnp.where(rev, seq_len, row0), q)
        a_cs = acs_ref[slot, su]
        a_csT = acsT_ref[slot, su]
        s_cat = scat_ref[slot, su]

        for g in range(groups):
            gc = slice(g * gw, (g + 1) * gw)
            ex = _dot(s_cat, expand[:, gc])
            dt_x, e_x, dec_x = ex[0:q], ex[q:2 * q], ex[2 * q:3 * q]
            xg = xbc_ref[rows, gc].astype(F32) * dt_x
            xg_lo = jnp.where(low_half, xg, 0.0).astype(BF16)
            xg_hi = jnp.where(low_half, 0.0, xg).astype(BF16)
            xdb = (xg * dec_x).astype(BF16)
            bg = xbc_ref[rows, db + g * nstate:db + (g + 1) * nstate]
            cg = xbc_ref[rows, db + gn + g * nstate:db + gn + (g + 1) * nstate]
            cb = lax.dot_general(cg, bg, (((1,), (1,)), ((), ())), preferred_element_type=F32)
            yo = _dot(cg, st_ref[:, gc].astype(BF16)) * e_x
            for pr in range(hpg // 2):
                h0 = g * hpg + 2 * pr
                ls = []
                for h in (h0, h0 + 1):
                    d = a_cs[:, h:h + 1] - a_csT[h:h + 1, :]
                    ls.append(jnp.where(mask, jnp.exp2(d), 0.0) * cb)
                lhs = jnp.concatenate(ls, axis=1).astype(BF16)
                pc = slice(2 * pr * hdim, (2 * pr + 2) * hdim)
                rhs = jnp.concatenate([xg_lo[:, pc], xg_hi[:, pc]], axis=0)
                yp = _dot(lhs, rhs) + yo[:, pc]
                cols = slice(h0 * hdim, (h0 + 2) * hdim)
                yf_ref[pl.ds(dst0, q), cols] = yp.astype(yf_ref.dtype)
                y_ref[rows, cols] = ((yf_ref[pl.ds(row0, q), cols].astype(F32) + yp)
                                     + dsk_ref[:, cols] * xbc_ref[rows, cols].astype(F32)).astype(y_ref.dtype)
            new_state = lax.dot_general(bg, xdb, (((0,), (0,)), ((), ())), preferred_element_type=F32)
            st_ref[:, gc] = st_ref[:, gc] * dsel(e_x[q - 1:q, :], e_x[0:1, :]) + new_state

    stage(dtTn_ref, t + 1 >= nc, 1 - slot)

    if want_state:
        @pl.when((t == nc - 1) | (t == 2 * nc - 1))
        def _():
            so_ref[...] = st_ref[...].T


def _ssd_call(xbc, dtT, dt_biasT, a_logT, d_skip_x, layer, seq_len, heads, hdim, nstate, h0T=None, state_out=None):
    t, cc = xbc.shape
    q = SSD_CHUNK
    bq = _tile(seq_len, SSD_STEP_CHUNKS * q)
    nsub = bq // q
    nc = seq_len // bq
    nseq = t // seq_len
    db = heads * hdim
    h2 = 2 * heads
    has_h0 = h0T is not None
    want_state = state_out is not None
    n_even = dt_biasT.shape[0]

    def chunk(tt):
        return jnp.where(tt >= nc, 2 * nc - 1 - tt, tt)

    def late(tt):
        return jnp.where(tt >= nc, 2 * nc - 1 - tt, nc - 1)

    def direction(tt):
        return jnp.where(tt >= nc, 1, 0)

    in_specs = [
        pl.BlockSpec((bq, cc), lambda s, tt: (s * nc + chunk(tt), 0)),
        pl.BlockSpec((h2, bq), lambda s, tt: (0, s * nc + chunk(tt))),
        pl.BlockSpec((h2, bq), lambda s, tt: (0, s * nc + chunk(jnp.minimum(tt + 1, 2 * nc - 1)))),
        pl.BlockSpec((None, h2, 1), lambda s, tt: (layer, 0, 0)),
        pl.BlockSpec((None, h2, 1), lambda s, tt: (layer, 0, 0)),
        pl.BlockSpec((None, 1, db), lambda s, tt: (layer, 0, 0)),
    ]
    args = [xbc, dtT, dtT, dt_biasT, a_logT, d_skip_x]
    if has_h0:
        in_specs.append(pl.BlockSpec((None, None, None, nstate, db),
                                     lambda s, tt: (s, layer, direction(tt), 0, 0)))
        args.append(h0T)
    out_specs = [pl.BlockSpec((bq, db), lambda s, tt: (s * nc + late(tt), 0))]
    out_shape = [jax.ShapeDtypeStruct((t, db), BF16)]
    aliases = {}
    if want_state:
        out_specs.append(pl.BlockSpec((None, None, None, db, nstate),
                                      lambda s, tt: (s, layer, direction(tt), 0, 0)))
        out_shape.append(jax.ShapeDtypeStruct((nseq, n_even, 2, db, nstate), F32))
        in_specs.append(pl.BlockSpec(memory_space=pl.ANY))
        args.append(state_out)
        aliases = {len(args) - 1: 1}
    kern = functools.partial(_ssd_kernel, nc=nc, heads=heads, hdim=hdim, groups=SSD_GROUPS, nstate=nstate,
                             has_h0=has_h0, want_state=want_state)
    if aliases:
        inner = kern
        n_in = len(args)

        def kern(*refs):
            return inner(*refs[:n_in - 1], *refs[n_in:])

    return pl.pallas_call(
        kern,
        grid=(nseq, 2 * nc),
        in_specs=in_specs,
        out_specs=out_specs,
        out_shape=out_shape,
        input_output_aliases=aliases,
        scratch_shapes=[
            pltpu.VMEM((nstate, db), F32),
            pltpu.VMEM((seq_len + q, db), BF16),
            pltpu.VMEM((2, nsub, q, heads), F32),
            pltpu.VMEM((2, nsub, heads, q), F32),
            pltpu.VMEM((2, nsub, 3 * q, h2), BF16),
        ],
        compiler_params=_params("parallel", "arbitrary"),
        name="ssd",
    )(*args)


def _mix_even_kernel(u_ref, v_ref, z_ref, ys_ref, ws_ref, bsT_ref, vn_ref, nw_ref, w_ref, x_ref, mod_ref, o_ref,
                     b_ref, *, groups):
    tm, d_a = u_ref.shape
    d_b = ys_ref.shape[1]
    ch = d_a // groups
    assert d_a == d_b
    gz = ys_ref[...].astype(F32) * _silu(z_ref[...].astype(F32))
    ms = jnp.mean(gz * gz, axis=-1, keepdims=True)
    b_ref[...] = (gz * lax.rsqrt(ms + EPS) * nw_ref[...]).astype(BF16)
    acc = None
    for g in range(groups):
        sl = slice(g * ch, (g + 1) * ch)
        part = _dot(b_ref[:, sl], w_ref[d_a + g * ch:d_a + (g + 1) * ch, :])
        acc = part if acc is None else acc + part
        vns = []
        for c0 in range(0, tm, CHUNK):
            v = _gelu_tanh(v_ref[c0:c0 + CHUNK, sl].astype(F32))
            ms = jnp.mean(v * v, axis=-1, keepdims=True)
            vns.append((v * lax.rsqrt(ms + EPS) * vn_ref[:, sl]).astype(BF16))
        sv = _dot(ws_ref[g], jnp.concatenate(vns, axis=1)) + bsT_ref[:, g:g + 1]
        parts = [(_gelu_tanh(u_ref[c * CHUNK:(c + 1) * CHUNK, sl].astype(F32)) * sv[:, c * ch:(c + 1) * ch])
                 .astype(BF16) for c in range(tm // CHUNK)]
        acc = acc + _dot(jnp.concatenate(parts, axis=0), w_ref[sl, :])
    o_ref[...] = x_ref[...] + mod_ref[2:3, :] * acc


def _mix_even_call(proj, ysum, z_block, ws, bsT, vnorm, norm_w, w, x, mods, layer, wl, seq_len):
    t, d = x.shape
    d_b = ysum.shape[1]
    d_a = w.shape[1] - d_b
    groups = ws.shape[1]
    assert d_a == d_b
    tm, cid = _row_tile(t, seq_len, mods.shape[1], 2 * CHUNK)
    return pl.pallas_call(
        functools.partial(_mix_even_kernel, groups=groups),
        grid=(t // tm,),
        in_specs=[
            pl.BlockSpec((tm, d_a), lambda i: (i, 0)),
            pl.BlockSpec((tm, d_a), lambda i: (i, 1)),
            pl.BlockSpec((tm, d_b), lambda i: (i, z_block)),
            pl.BlockSpec((tm, d_b), lambda i: (i, 0)),
            pl.BlockSpec((None, groups, CHUNK, CHUNK), lambda i: (wl, 0, 0, 0)),
            pl.BlockSpec((None, CHUNK, groups), lambda i: (wl, 0, 0)),
            pl.BlockSpec((None, 1, d_a), lambda i: (wl, 0, 0)),
            pl.BlockSpec((None, 1, d_b), lambda i: (wl, 0, 0)),
            pl.BlockSpec((None, d_a + d_b, d), lambda i: (wl, 0, 0), pipeline_mode=pl.Buffered(1)),
            pl.BlockSpec((tm, d), lambda i: (i, 0)),
            pl.BlockSpec((None, None, N_MOD, d), lambda i: (layer, cid(i), 0, 0)),
        ],
        out_specs=pl.BlockSpec((tm, d), lambda i: (i, 0)),
        out_shape=jax.ShapeDtypeStruct((t, d), F32),
        scratch_shapes=[pltpu.VMEM((tm, d_b), BF16)],
        compiler_params=_params("parallel"),
        name="mix_even",
    )(proj, proj, proj, ysum, ws, bsT, vnorm, norm_w, w, x, mods)


def _pool_kernel(hc_ref, pw_ref, ps_ref, wo_ref, x_ref, mod_ref, o_ref, *, win, sub):
    tm, d_c = hc_ref.shape
    ng = len(POOL_WINDOWS)
    pch = d_c // ng
    assert win & (win - 1) == 0
    ri = lax.broadcasted_iota(I32, (sub, sub), 0)
    ci = lax.broadcasted_iota(I32, (sub, sub), 1)
    same = lax.shift_right_logical(ri, win.bit_length() - 1) == lax.shift_right_logical(ci, win.bit_length() - 1)
    pos = lax.broadcasted_iota(I32, (sub, 1), 0) & (win - 1)
    mixed = []
    for g, k in enumerate(POOL_WINDOWS):
        off = ci - ri
        band = jnp.where(same & (off >= -(k // 2)) & (off < k - k // 2), 1.0, 0.0).astype(BF16)
        lo = jnp.maximum(pos - k // 2, 0)
        hi = jnp.minimum(pos - k // 2 + k, win)
        cnt = (hi - lo).astype(F32)
        pooled = []
        for r in range(tm // sub):
            xb = hc_ref[r * sub:(r + 1) * sub, g * pch:(g + 1) * pch]
            pooled.append(_dot(band, xb) / cnt - xb.astype(F32))
        pg = jnp.concatenate(pooled, axis=0).astype(BF16)
        mixed.append(_dot(pg, pw_ref[g]) * ps_ref[:, g * pch:(g + 1) * pch])
    mix = jnp.concatenate(mixed, axis=1).astype(BF16)
    o_ref[...] = x_ref[...] + mod_ref[2:3, :] * _dot(mix, wo_ref[...])


def _pool_call(hc, pool_w, pool_scale, w_out, x, mods, layer, wl, seq_len, win):
    t, d = x.shape
    d_c = hc.shape[1]
    _, ng, pch, _ = pool_w.shape
    sub = max(win, 128)
    assert sub % win == 0 and seq_len % sub == 0
    tm, cid = _row_tile(t, seq_len, mods.shape[1], 512)
    assert tm % sub == 0
    return pl.pallas_call(
        functools.partial(_pool_kernel, win=win, sub=sub),
        grid=(t // tm,),
        in_specs=[
            pl.BlockSpec((tm, d_c), lambda i: (i, 0)),
            pl.BlockSpec((None, ng, pch, pch), lambda i: (wl, 0, 0, 0)),
            pl.BlockSpec((None, 1, d_c), lambda i: (wl, 0, 0)),
            pl.BlockSpec((None, d_c, d), lambda i: (wl, 0, 0)),
            pl.BlockSpec((tm, d), lambda i: (i, 0)),
            pl.BlockSpec((None, None, N_MOD, d), lambda i: (layer, cid(i), 0, 0)),
        ],
        out_specs=pl.BlockSpec((tm, d), lambda i: (i, 0)),
        out_shape=jax.ShapeDtypeStruct((t, d), F32),
        compiler_params=_params("parallel"),
        name="pool_out",
    )(hc, pool_w, pool_scale, w_out, x, mods)


def _ffn_kernel(x_ref, mod_ref, nw_ref, w1_ref, w3_ref, w2_ref, fn_ref, o_ref, h_ref, *, nf, final, dn):
    f = pl.program_id(1)
    tm, d = o_ref.shape

    def step(h, last):
        act = (_silu(_dot(h, w1_ref[...])) * _dot(h, w3_ref[...])).astype(BF16)
        if not (final and last):
            for n0 in range(0, d, dn):
                o_ref[:, n0:n0 + dn] += mod_ref[5:6, n0:n0 + dn] * _dot(act, w2_ref[:, n0:n0 + dn])
            return
        rb = min(FINAL_NORM_ROWS, tm)
        for r0 in range(0, tm, rb):
            y = o_ref[r0:r0 + rb, :] + mod_ref[5:6, :] * _dot(act[r0:r0 + rb], w2_ref[...])
            ms = jnp.mean(y * y, axis=-1, keepdims=True)
            o_ref[r0:r0 + rb, :] = y * lax.rsqrt(ms + EPS) * fn_ref[...]

    def run(first, last):
        if first:
            step(_norm_mod_rows(x_ref, nw_ref, mod_ref[3:4, :], mod_ref[4:5, :], h_ref, copy_ref=o_ref), last)
        else:
            step(h_ref[...], last)

    if nf == 1:
        run(True, True)
    elif not final:
        pl.when(f == 0)(lambda: run(True, False))
        pl.when(f > 0)(lambda: run(False, False))
    else:
        pl.when(f == 0)(lambda: run(True, False))
        pl.when(f == nf - 1)(lambda: run(False, True))
        if nf > 2:
            pl.when((f > 0) & (f < nf - 1))(lambda: run(False, False))


def _ffn_call(x, mods, norm_w, w1, w3, w2, final_nw, layer, seq_len, final):
    t, d = x.shape
    dff = w1.shape[2]
    tm, cid = _row_tile(t, seq_len, mods.shape[1], 1024)
    tf = _tile(dff, 512)
    nf = dff // tf
    x_spec = pl.BlockSpec((tm, d), lambda i, f: (i, 0))
    return pl.pallas_call(
        functools.partial(_ffn_kernel, nf=nf, final=final, dn=_tile(d, 512)),
        grid=(t // tm, nf),
        in_specs=[
            x_spec,
            pl.BlockSpec((None, None, N_MOD, d), lambda i, f: (layer, cid(i), 0, 0)),
            pl.BlockSpec((None, 1, d), lambda i, f: (layer, 0, 0)),
            pl.BlockSpec((None, d, tf), lambda i, f: (layer, 0, f)),
            pl.BlockSpec((None, d, tf), lambda i, f: (layer, 0, f)),
            pl.BlockSpec((None, tf, d), lambda i, f: (layer, f, 0)),
            pl.BlockSpec((1, d), lambda i, f: (0, 0)),
        ],
        out_specs=pl.BlockSpec((tm, d), lambda i, f: (i, 0)),
        out_shape=jax.ShapeDtypeStruct((t, d), F32),
        scratch_shapes=[pltpu.VMEM((tm, d), BF16)],
        compiler_params=_params("parallel", "arbitrary"),
        name="ffn",
    )(x, mods, norm_w, w1, w3, w2, final_nw)


def _run_group(x, seq_len, mods, win, h0T, want_state, p):
    depth = mods.shape[0]
    state = None
    if want_state:
        n_even = p["ssd_dt_biasT"].shape[0]
        state = jnp.zeros((x.shape[0] // seq_len, n_even, 2, p["heads"] * p["hdim"], p["nstate"]), F32)
    for i in range(depth):
        if i % 2 == 0:
            e = i // 2
            proj, dtT = _in_call(x, mods, p["norm_mix"], p["w_in_even"], i, e, p["n_main"], seq_len, 2304,
                                 p["w_in_dtT"])
            xbc = _conv_call(proj, p["ssd_conv_w"], p["ssd_conv_b"], e, seq_len, p["xbc_block"])
            res = _ssd_call(xbc, dtT, p["ssd_dt_biasT"], p["ssd_a_logT"], p["ssd_d_x"], e, seq_len,
                            p["heads"], p["hdim"], p["nstate"], h0T, state)
            if want_state:
                state = res[1]
            x = _mix_even_call(proj, res[0], p["z_block"], p["gmlp_ws"], p["gmlp_bsT"], p["gmlp_norm"],
                               p["ssd_norm"], p["w_out_even"], x, mods, i, e, seq_len)
        else:
            o = i // 2
            (hc,) = _in_call(x, mods, p["norm_mix"], p["w_in_odd"], i, o, p["w_in_odd"].shape[2], seq_len, 2048)
            x = _pool_call(hc, p["pool_w"], p["pool_scale"], p["w_out_odd"], x, mods, i, o, seq_len, win)
        x = _ffn_call(x, mods, p["norm_ffn"], p["ffn_w1"], p["ffn_w3"], p["ffn_w2"], p["final_norm"], i, seq_len,
                      final=(i == depth - 1))
    return x, state


def kernel(x_prompt, x_sample, state_ssd, c, c_ctx, w_mod, b_mod, norm_mix, norm_ffn, w_in_even, w_out_even, gmlp_norm, gmlp_ws, gmlp_bs, ssd_conv_w, ssd_conv_b, ssd_dt_bias, ssd_a_log, ssd_d, ssd_norm, w_in_odd, pool_w, pool_scale, w_out_odd, ffn_w1, ffn_w3, ffn_w2, final_norm):
    batch, seq, d = x_prompt.shape
    dec_batch, dec_seq, _ = x_sample.shape
    depth = w_mod.shape[0]
    n_even = w_in_even.shape[0]
    heads, hdim, nstate = state_ssd.shape[3:]
    d_b = heads * hdim
    d_a = gmlp_norm.shape[1]
    cc = ssd_conv_w.shape[2]
    n_main = 2 * d_a + d_b + cc
    h2 = 2 * heads
    assert w_in_even.shape[2] == n_main + h2 and h2 <= LANES
    assert (2 * d_a) % d_b == 0 and (n_main - cc) % cc == 0

    n_cond = 1 + dec_batch
    rows = -(-n_cond // 8) * 8
    cond = jnp.concatenate([c_ctx[None], c, jnp.zeros((rows - n_cond, d), F32)], axis=0)
    mods = _mod_call(cond, w_mod, b_mod).reshape(depth, rows, N_MOD, d)

    p = dict(
        heads=heads, hdim=hdim, nstate=nstate, n_main=n_main,
        z_block=(2 * d_a) // d_b, xbc_block=(n_main - cc) // cc,
        norm_mix=norm_mix.reshape(depth, 1, d), norm_ffn=norm_ffn.reshape(depth, 1, d),
        w_in_even=w_in_even.astype(BF16),
        w_in_dtT=jnp.swapaxes(w_in_even[:, :, n_main:], 1, 2).astype(BF16),
        w_out_even=w_out_even.astype(BF16),
        gmlp_norm=gmlp_norm.reshape(n_even, 1, d_a), gmlp_ws=gmlp_ws.astype(BF16),
        gmlp_bsT=jnp.swapaxes(gmlp_bs, 1, 2),
        ssd_conv_w=ssd_conv_w, ssd_conv_b=ssd_conv_b.reshape(n_even, 1, cc),
        ssd_dt_biasT=ssd_dt_bias.reshape(n_even, h2, 1), ssd_a_logT=ssd_a_log.reshape(n_even, h2, 1),
        ssd_d_x=jnp.repeat(ssd_d, hdim, axis=1).reshape(n_even, 1, d_b),
        ssd_norm=ssd_norm.reshape(n_even, 1, d_b),
        w_in_odd=w_in_odd.astype(BF16), pool_w=pool_w.astype(BF16),
        pool_scale=pool_scale.reshape(-1, 1, d), w_out_odd=w_out_odd.astype(BF16),
        ffn_w1=ffn_w1.astype(BF16), ffn_w3=ffn_w3.astype(BF16), ffn_w2=ffn_w2.astype(BF16),
        final_norm=final_norm.reshape(1, d),
    )

    y_p, state_new = _run_group(x_prompt.reshape(batch * seq, d), seq, mods[:, 0:1], seq, None, True, p)
    h0T = jnp.swapaxes(state_ssd.reshape(dec_batch, n_even, 2, d_b, nstate), 3, 4)
    y_s, _ = _run_group(x_sample.reshape(dec_batch * dec_seq, d), dec_seq, mods[:, 1:n_cond], GRID_W, h0T, False, p)

    return (y_p.reshape(batch, seq, d), y_s.reshape(dec_batch, dec_seq, d),
            state_new.reshape(batch, n_even, 2, heads, hdim, nstate))
```

```python
import functools
import math

import jax
import jax.numpy as jnp
from jax import lax
from jax.experimental import pallas as pl
from jax.experimental.pallas import tpu as pltpu

F32 = jnp.float32
BF16 = jnp.bfloat16
I32 = jnp.int32

EPS = 1e-6
N_MOD = 6
GRID_W = 64
CHUNK = 128
SSD_CHUNK = 128
SSD_STEP_CHUNKS = 4
SSD_GROUPS = 4
D_CONV = 5
POOL_WINDOWS = (2, 4, 8, 16)
LANES = 128
HALO = 16
NORM_ROWS = 32
FINAL_NORM_ROWS = 256
CONV_COLS = 512
CAST_BLOCK_BYTES = 2 * 1024 * 1024
LOG2E = 1.4426950408889634
VMEM_LIMIT_BYTES = 56 * 1024 * 1024


def _params(*sem):
    return pltpu.CompilerParams(dimension_semantics=sem, vmem_limit_bytes=VMEM_LIMIT_BYTES)


def _tile(n, pref):
    t = min(n, pref)
    assert n % t == 0, (n, pref)
    return t


def _row_tile(t, seq_len, n_cond, pref):
    if n_cond == 1:
        return _tile(t, pref), (lambda i: 0)
    tm = _tile(seq_len, min(pref, seq_len))
    return tm, (lambda i: (i * tm) // seq_len)


def _silu(x):
    return x * jax.nn.sigmoid(x)


def _softplus(x):
    return jnp.maximum(x, 0.0) + jnp.log1p(jnp.exp(-jnp.abs(x)))


def _split3(x):
    x1 = x.astype(BF16)
    r = x - x1.astype(F32)
    x2 = r.astype(BF16)
    x3 = (r - x2.astype(F32)).astype(BF16)
    return x1, x2, x3


def _dot(a, b):
    return jnp.dot(a, b, preferred_element_type=F32)


def _norm_mod_rows(x_ref, nw_ref, shift, scale, h_ref, copy_ref=None):
    gain = nw_ref[...] * (1.0 + scale)
    tm = x_ref.shape[0]
    rows = min(NORM_ROWS, tm)
    blocks = []
    for r in range(0, tm, rows):
        x = x_ref[r:r + rows, :]
        if copy_ref is not None:
            copy_ref[r:r + rows, :] = x
        ms = jnp.mean(x * x, axis=-1, keepdims=True)
        hb = (x * lax.rsqrt(ms + EPS) * gain + shift).astype(h_ref.dtype)
        h_ref[r:r + rows, :] = hb
        blocks.append(hb)
    return jnp.concatenate(blocks, axis=0)


def _mod_kernel(c_ref, w_ref, b_ref, o_ref):
    s = _silu(c_ref[...])
    w = w_ref[...]
    s1 = s.astype(BF16)
    s2 = (s - s1.astype(F32)).astype(BF16)
    w1 = w.astype(BF16)
    w2 = (w - w1.astype(F32)).astype(BF16)
    o_ref[...] = (_dot(s1, w1) + _dot(s2, w1) + _dot(s1, w2)) + b_ref[...]


def _mod_call(cond, w_mod, b_mod):
    depth, d, n = w_mod.shape
    rows = cond.shape[0]
    tn = _tile(n, 1024)
    return pl.pallas_call(
        _mod_kernel,
        grid=(depth, n // tn),
        in_specs=[
            pl.BlockSpec((rows, d), lambda l, j: (0, 0)),
            pl.BlockSpec((None, d, tn), lambda l, j: (l, 0, j)),
            pl.BlockSpec((None, 1, tn), lambda l, j: (l, 0, j)),
        ],
        out_specs=pl.BlockSpec((None, rows, tn), lambda l, j: (l, 0, j)),
        out_shape=jax.ShapeDtypeStruct((depth, rows, n), F32),
        compiler_params=_params("parallel", "parallel"),
        name="mod",
    )(cond, w_mod, b_mod.reshape(depth, 1, n))


def _in_kernel(x_ref, mod_ref, nw_ref, w_ref, *rest, with_dt):
    if with_dt:
        wdtT_ref, o_ref, dtT_ref, h_ref = rest
    else:
        o_ref, h_ref = rest

    j = pl.program_id(1)

    @pl.when(j == 0)
    def _():
        h = _norm_mod_rows(x_ref, nw_ref, mod_ref[0:1, :], mod_ref[1:2, :], h_ref)
        o_ref[...] = _dot(h, w_ref[...]).astype(o_ref.dtype)
        if with_dt:
            dtT_ref[...] = lax.dot_general(wdtT_ref[...], h, (((1,), (1,)), ((), ())),
                                           preferred_element_type=F32)

    @pl.when(j > 0)
    def _():
        o_ref[...] = _dot(h_ref[...], w_ref[...]).astype(o_ref.dtype)


def _in_call(x, mods, norm_w, w, layer, wl, n, seq_len, tn_pref, wdtT=None):
    t, d = x.shape
    tm, cid = _row_tile(t, seq_len, mods.shape[1], 1024)
    tn = _tile(n, tn_pref)
    with_dt = wdtT is not None
    in_specs = [
        pl.BlockSpec((tm, d), lambda i, j: (i, 0)),
        pl.BlockSpec((None, None, N_MOD, d), lambda i, j: (layer, cid(i), 0, 0)),
        pl.BlockSpec((None, 1, d), lambda i, j: (layer, 0, 0)),
        pl.BlockSpec((None, d, tn), lambda i, j: (wl, 0, j)),
    ]
    out_specs = [pl.BlockSpec((tm, tn), lambda i, j: (i, j))]
    out_shape = [jax.ShapeDtypeStruct((t, n), BF16)]
    args = [x, mods, norm_w, w]
    if with_dt:
        h2 = wdtT.shape[1]
        in_specs.append(pl.BlockSpec((None, h2, d), lambda i, j: (wl, 0, 0)))
        out_specs.append(pl.BlockSpec((h2, tm), lambda i, j: (0, i)))
        out_shape.append(jax.ShapeDtypeStruct((h2, t), F32))
        args.append(wdtT)
    return pl.pallas_call(
        functools.partial(_in_kernel, with_dt=with_dt),
        grid=(t // tm, n // tn),
        in_specs=in_specs,
        out_specs=out_specs,
        out_shape=out_shape,
        scratch_shapes=[pltpu.VMEM((tm, d), BF16)],
        compiler_params=_params("parallel", "arbitrary"),
        name="in_proj",
    )(*args)


def _gelu_tanh(x):
    k = math.sqrt(2.0 / math.pi)
    a = -2.0 * k * 0.044715 * LOG2E
    b = -2.0 * k * LOG2E
    return x / (1.0 + jnp.exp2(x * (x * x * a + b)))


def _cast_blocks(rows, nsteps):
    for nb in range(min(nsteps, rows), 0, -1):
        if rows % nb == 0 and (rows // nb) % 16 == 0:
            return nb
    return None


def _conv_kernel(cur_ref, prev_ref, next_ref, w_ref, b_ref, *rest, nc, n_cast):
    cast_in, o_ref, cast_out = rest[:n_cast], rest[n_cast], rest[n_cast + 1:]
    for src, dst in zip(cast_in, cast_out):
        dst[...] = src[...].astype(dst.dtype)

    c = pl.program_id(1)
    bq, cc = cur_ref.shape
    q = min(bq, SSD_CHUNK)
    rows = q + 2 * HALO
    ri = lax.broadcasted_iota(I32, (q, rows), 0)
    ci = lax.broadcasted_iota(I32, (q, rows), 1)
    picks = [None if k == D_CONV // 2 else
             jnp.where(ci == ri + (HALO + k - D_CONV // 2), 1.0, 0.0).astype(BF16) for k in range(D_CONV)]
    cw = min(CONV_COLS, cc)
    nsub = bq // q
    for n0 in range(0, cc, cw):
        cs = slice(n0, n0 + cw)
        prev = jnp.where(c > 0, prev_ref[:, cs], jnp.zeros((HALO, cw), BF16))
        nxt = jnp.where(c < nc - 1, next_ref[:, cs], jnp.zeros((HALO, cw), BF16))
        for u in range(nsub):
            cur = cur_ref[u * q:(u + 1) * q, cs]
            before = prev if u == 0 else cur_ref[u * q - HALO:u * q, cs]
            after = nxt if u == nsub - 1 else cur_ref[(u + 1) * q:(u + 1) * q + HALO, cs]
            ext = jnp.concatenate([before, cur, after], axis=0)
            acc = b_ref[:, cs]
            for k in range(D_CONV):
                tap = cur.astype(F32) if picks[k] is None else _dot(picks[k], ext)
                acc = acc + w_ref[k:k + 1, cs] * tap
            o_ref[u * q:(u + 1) * q, cs] = _silu(acc).astype(o_ref.dtype)


def _conv_steps(t, seq_len):
    q = _tile(seq_len, 2 * SSD_CHUNK)
    return q, (t // seq_len) * (seq_len // q)


def _conv_call(proj, conv_w, conv_b, layer, seq_len, col_block, cast=None):
    t = proj.shape[0]
    cc = conv_w.shape[2]
    q, nsteps = _conv_steps(t, seq_len)
    nc = seq_len // q
    nseq = t // seq_len
    per = q // HALO
    last = t // HALO - 1
    in_specs = [
        pl.BlockSpec((q, cc), lambda s, c: (s * nc + c, col_block)),
        pl.BlockSpec((HALO, cc), lambda s, c: (jnp.maximum((s * nc + c) * per - 1, 0), col_block)),
        pl.BlockSpec((HALO, cc), lambda s, c: (jnp.minimum((s * nc + c + 1) * per, last), col_block)),
        pl.BlockSpec((None, D_CONV, cc), lambda s, c: (layer, 0, 0)),
        pl.BlockSpec((None, 1, cc), lambda s, c: (layer, 0, 0)),
    ]
    args = [proj, proj, proj, conv_w, conv_b]
    out_specs = [pl.BlockSpec((q, cc), lambda s, c: (s * nc + c, 0))]
    out_shape = [jax.ShapeDtypeStruct((t, cc), BF16)]
    n_cast = 0
    if cast is not None:
        weights, wl = cast
        for w in weights:
            rows, cols = w.shape[1:]
            nb = _cast_blocks(rows, nsteps)
            blk = functools.partial(lambda s, c, nb: jnp.minimum(s * nc + c, nb - 1), nb=nb)
            in_specs.append(pl.BlockSpec((None, rows // nb, cols), lambda s, c, blk=blk: (wl, blk(s, c), 0)))
            out_specs.append(pl.BlockSpec((rows // nb, cols), lambda s, c, blk=blk: (blk(s, c), 0)))
            out_shape.append(jax.ShapeDtypeStruct((rows, cols), BF16))
            args.append(w)
        n_cast = len(weights)
    res = pl.pallas_call(
        functools.partial(_conv_kernel, nc=nc, n_cast=n_cast),
        grid=(nseq, nc),
        in_specs=in_specs,
        out_specs=out_specs,
        out_shape=out_shape,
        compiler_params=_params("arbitrary", "arbitrary"),
        name="conv",
    )(*args)
    return res[0], tuple(res[1:])


def _ssd_factors(dtT_ref, lanes, dtbT_ref, alogT_ref, rev, heads):
    q = lanes.stop - lanes.start

    def dsel(fwd, bwd):
        return jnp.where(rev, bwd, fwd)

    dtT = _softplus(dsel(dtT_ref[0:heads, lanes], dtT_ref[heads:2 * heads, lanes])
                    + dsel(dtbT_ref[0:heads, :], dtbT_ref[heads:2 * heads, :]))
    aT = dtT * (-jnp.exp(dsel(alogT_ref[0:heads, :], alogT_ref[heads:2 * heads, :]))) * LOG2E
    ri = lax.broadcasted_iota(I32, (q, q), 0)
    ci = lax.broadcasted_iota(I32, (q, q), 1)
    triT = dsel(jnp.where(ci >= ri, 1.0, 0.0), jnp.where(ri >= ci, 1.0, 0.0)).astype(BF16)
    b1, b2, b3 = _split3(aT)
    a_csT = (_dot(b1, triT) + _dot(b2, triT)) + _dot(b3, triT)
    a_cs = a_csT.T
    tot = dsel(a_cs[q - 1:q, :], a_cs[0:1, :])
    stack = jnp.concatenate([dtT.T, jnp.exp2(a_cs), jnp.exp2(tot - a_cs)], axis=0)
    s_hi = stack.astype(BF16)
    s_lo = (stack - s_hi.astype(F32)).astype(BF16)
    return a_cs, a_csT, jnp.concatenate([s_hi, s_lo], axis=1)


def _ssd_kernel(xbc_ref, dtT_ref, dtTn_ref, dtbT_ref, alogT_ref, dsk_ref,
                *rest, nc, heads, hdim, groups, nstate, has_h0, want_state):
    rest = list(rest)
    h0_ref = so_ref = None
    if has_h0:
        h0_ref = rest.pop(0)
    y_ref = rest.pop(0)
    if want_state:
        so_ref = rest.pop(0)
    st_ref, yf_ref, acs_ref, acsT_ref, scat_ref = rest

    q = SSD_CHUNK
    bq = xbc_ref.shape[0]
    nsub = bq // q
    db = heads * hdim
    gn = groups * nstate
    hpg = heads // groups
    gw = hpg * hdim
    seq_len = nc * bq
    t = pl.program_id(1)
    rev = t >= nc
    c = jnp.where(rev, 2 * nc - 1 - t, t)
    slot = t & 1

    def dsel(fwd, bwd):
        return jnp.where(rev, bwd, fwd)

    def stage(src_ref, is_rev, dst):
        for s in range(nsub):
            a_cs, a_csT, s_cat = _ssd_factors(src_ref, slice(s * q, (s + 1) * q), dtbT_ref, alogT_ref, is_rev,
                                              heads)
            acs_ref[dst, s] = a_cs
            acsT_ref[dst, s] = a_csT
            scat_ref[dst, s] = s_cat

    @pl.when(t == 0)
    def _():
        stage(dtT_ref, rev, slot)

    @pl.when((t == 0) | (t == nc))
    def _():
        st_ref[...] = h0_ref[...] if has_h0 else jnp.zeros_like(st_ref)

    ri = lax.broadcasted_iota(I32, (q, q), 0)
    ci = lax.broadcasted_iota(I32, (q, q), 1)
    mask = dsel(jnp.where(ri >= ci, 1.0, 0.0), jnp.where(ci >= ri, 1.0, 0.0)) > 0.5

    assert hdim & (hdim - 1) == 0 and heads & (heads - 1) == 0
    eh = lax.broadcasted_iota(I32, (2 * heads, db), 0) & (heads - 1)
    ec = lax.shift_right_logical(lax.broadcasted_iota(I32, (2 * heads, db), 1), hdim.bit_length() - 1)
    expand = jnp.where(eh == ec, 1.0, 0.0).astype(BF16)
    low_half = (lax.broadcasted_iota(I32, (q, gw), 1) & hdim) == 0

    for u in range(nsub):
        su = jnp.where(rev, nsub - 1 - u, u) if nsub > 1 else 0
        rows = pl.ds(pl.multiple_of(su * q, q), q) if nsub > 1 else slice(None)
        row0 = pl.multiple_of(c * bq + su * q, q)
        dst0 = pl.multiple_of(jnp.where(rev, seq_len, row0), q)
        a_cs = acs_ref[slot, su]
        a_csT = acsT_ref[slot, su]
        s_cat = scat_ref[slot, su]

        for g in range(groups):
            gc = slice(g * gw, (g + 1) * gw)
            ex = _dot(s_cat, expand[:, gc])
            dt_x, e_x, dec_x = ex[0:q], ex[q:2 * q], ex[2 * q:3 * q]
            xg = xbc_ref[rows, gc].astype(F32) * dt_x
            xg_lo = jnp.where(low_half, xg, 0.0).astype(BF16)
            xg_hi = jnp.where(low_half, 0.0, xg).astype(BF16)
            xdb = (xg * dec_x).astype(BF16)
            bg = xbc_ref[rows, db + g * nstate:db + (g + 1) * nstate]
            cg = xbc_ref[rows, db + gn + g * nstate:db + gn + (g + 1) * nstate]
            cb = lax.dot_general(cg, bg, (((1,), (1,)), ((), ())), preferred_element_type=F32)
            yo = _dot(cg, st_ref[:, gc].astype(BF16)) * e_x
            for pr in range(hpg // 2):
                h0 = g * hpg + 2 * pr
                ls = []
                for h in (h0, h0 + 1):
                    d = a_cs[:, h:h + 1] - a_csT[h:h + 1, :]
                    ls.append(jnp.where(mask, jnp.exp2(d), 0.0) * cb)
                lhs = jnp.concatenate(ls, axis=1).astype(BF16)
                pc = slice(2 * pr * hdim, (2 * pr + 2) * hdim)
                rhs = jnp.concatenate([xg_lo[:, pc], xg_hi[:, pc]], axis=0)
                yp = _dot(lhs, rhs) + yo[:, pc]
                cols = slice(h0 * hdim, (h0 + 2) * hdim)
                yf_ref[pl.ds(dst0, q), cols] = yp.astype(yf_ref.dtype)
                y_ref[rows, cols] = ((yf_ref[pl.ds(row0, q), cols].astype(F32) + yp)
                                     + dsk_ref[:, cols] * xbc_ref[rows, cols].astype(F32)).astype(y_ref.dtype)
            new_state = lax.dot_general(bg, xdb, (((0,), (0,)), ((), ())), preferred_element_type=F32)
            st_ref[:, gc] = st_ref[:, gc] * dsel(e_x[q - 1:q, :], e_x[0:1, :]) + new_state

    stage(dtTn_ref, t + 1 >= nc, 1 - slot)

    if want_state:
        @pl.when((t == nc - 1) | (t == 2 * nc - 1))
        def _():
            so_ref[...] = st_ref[...].T


def _ssd_call(xbc, dtT, dt_biasT, a_logT, d_skip_x, layer, seq_len, heads, hdim, nstate, h0T=None, state_out=None):
    t, cc = xbc.shape
    q = SSD_CHUNK
    bq = _tile(seq_len, SSD_STEP_CHUNKS * q)
    nsub = bq // q
    nc = seq_len // bq
    nseq = t // seq_len
    db = heads * hdim
    h2 = 2 * heads
    has_h0 = h0T is not None
    want_state = state_out is not None
    n_even = dt_biasT.shape[0]

    def chunk(tt):
        return jnp.where(tt >= nc, 2 * nc - 1 - tt, tt)

    def late(tt):
        return jnp.where(tt >= nc, 2 * nc - 1 - tt, nc - 1)

    def direction(tt):
        return jnp.where(tt >= nc, 1, 0)

    in_specs = [
        pl.BlockSpec((bq, cc), lambda s, tt: (s * nc + chunk(tt), 0)),
        pl.BlockSpec((h2, bq), lambda s, tt: (0, s * nc + chunk(tt))),
        pl.BlockSpec((h2, bq), lambda s, tt: (0, s * nc + chunk(jnp.minimum(tt + 1, 2 * nc - 1)))),
        pl.BlockSpec((None, h2, 1), lambda s, tt: (layer, 0, 0)),
        pl.BlockSpec((None, h2, 1), lambda s, tt: (layer, 0, 0)),
        pl.BlockSpec((None, 1, db), lambda s, tt: (layer, 0, 0)),
    ]
    args = [xbc, dtT, dtT, dt_biasT, a_logT, d_skip_x]
    if has_h0:
        in_specs.append(pl.BlockSpec((None, None, None, nstate, db),
                                     lambda s, tt: (s, layer, direction(tt), 0, 0)))
        args.append(h0T)
    out_specs = [pl.BlockSpec((bq, db), lambda s, tt: (s * nc + late(tt), 0))]
    out_shape = [jax.ShapeDtypeStruct((t, db), BF16)]
    aliases = {}
    if want_state:
        out_specs.append(pl.BlockSpec((None, None, None, db, nstate),
                                      lambda s, tt: (s, layer, direction(tt), 0, 0)))
        out_shape.append(jax.ShapeDtypeStruct((nseq, n_even, 2, db, nstate), F32))
        in_specs.append(pl.BlockSpec(memory_space=pl.ANY))
        args.append(state_out)
        aliases = {len(args) - 1: 1}
    kern = functools.partial(_ssd_kernel, nc=nc, heads=heads, hdim=hdim, groups=SSD_GROUPS, nstate=nstate,
                             has_h0=has_h0, want_state=want_state)
    if aliases:
        inner = kern
        n_in = len(args)

        def kern(*refs):
            return inner(*refs[:n_in - 1], *refs[n_in:])

    return pl.pallas_call(
        kern,
        grid=(nseq, 2 * nc),
        in_specs=in_specs,
        out_specs=out_specs,
        out_shape=out_shape,
        input_output_aliases=aliases,
        scratch_shapes=[
            pltpu.VMEM((nstate, db), F32),
            pltpu.VMEM((seq_len + q, db), BF16),
            pltpu.VMEM((2, nsub, q, heads), F32),
            pltpu.VMEM((2, nsub, heads, q), F32),
            pltpu.VMEM((2, nsub, 3 * q, h2), BF16),
        ],
        compiler_params=_params("parallel", "arbitrary"),
        name="ssd",
    )(*args)


def _mix_even_kernel(u_ref, v_ref, z_ref, ys_ref, ws_ref, bsT_ref, vn_ref, nw_ref, w_ref, x_ref, mod_ref, o_ref,
                     b_ref, *, groups):
    tm, d_a = u_ref.shape
    d_b = ys_ref.shape[1]
    ch = d_a // groups
    assert d_a == d_b
    gz = ys_ref[...].astype(F32) * _silu(z_ref[...].astype(F32))
    ms = jnp.mean(gz * gz, axis=-1, keepdims=True)
    b_ref[...] = (gz * lax.rsqrt(ms + EPS) * nw_ref[...]).astype(BF16)
    acc = None
    for g in range(groups):
        sl = slice(g * ch, (g + 1) * ch)
        part = _dot(b_ref[:, sl], w_ref[d_a + g * ch:d_a + (g + 1) * ch, :])
        acc = part if acc is None else acc + part
        vns = []
        for c0 in range(0, tm, CHUNK):
            v = _gelu_tanh(v_ref[c0:c0 + CHUNK, sl].astype(F32))
            ms = jnp.mean(v * v, axis=-1, keepdims=True)
            vns.append((v * lax.rsqrt(ms + EPS) * vn_ref[:, sl]).astype(BF16))
        sv = _dot(ws_ref[g], jnp.concatenate(vns, axis=1)) + bsT_ref[:, g:g + 1]
        parts = [(_gelu_tanh(u_ref[c * CHUNK:(c + 1) * CHUNK, sl].astype(F32)) * sv[:, c * ch:(c + 1) * ch])
                 .astype(BF16) for c in range(tm // CHUNK)]
        acc = acc + _dot(jnp.concatenate(parts, axis=0), w_ref[sl, :])
    o_ref[...] = x_ref[...] + mod_ref[2:3, :] * acc


def _mix_even_call(proj, ysum, z_block, ws, bsT, vnorm, norm_w, w, x, mods, layer, wl, seq_len):
    t, d = x.shape
    d_b = ysum.shape[1]
    d_a = w.shape[1] - d_b
    groups = ws.shape[1]
    assert d_a == d_b
    tm, cid = _row_tile(t, seq_len, mods.shape[1], 2 * CHUNK)
    return pl.pallas_call(
        functools.partial(_mix_even_kernel, groups=groups),
        grid=(t // tm,),
        in_specs=[
            pl.BlockSpec((tm, d_a), lambda i: (i, 0)),
            pl.BlockSpec((tm, d_a), lambda i: (i, 1)),
            pl.BlockSpec((tm, d_b), lambda i: (i, z_block)),
            pl.BlockSpec((tm, d_b), lambda i: (i, 0)),
            pl.BlockSpec((None, groups, CHUNK, CHUNK), lambda i: (wl, 0, 0, 0)),
            pl.BlockSpec((None, CHUNK, groups), lambda i: (wl, 0, 0)),
            pl.BlockSpec((None, 1, d_a), lambda i: (wl, 0, 0)),
            pl.BlockSpec((None, 1, d_b), lambda i: (wl, 0, 0)),
            pl.BlockSpec((None, d_a + d_b, d), lambda i: (wl, 0, 0), pipeline_mode=pl.Buffered(1)),
            pl.BlockSpec((tm, d), lambda i: (i, 0)),
            pl.BlockSpec((None, None, N_MOD, d), lambda i: (layer, cid(i), 0, 0)),
        ],
        out_specs=pl.BlockSpec((tm, d), lambda i: (i, 0)),
        out_shape=jax.ShapeDtypeStruct((t, d), F32),
        scratch_shapes=[pltpu.VMEM((tm, d_b), BF16)],
        compiler_params=_params("parallel"),
        name="mix_even",
    )(proj, proj, proj, ysum, ws, bsT, vnorm, norm_w, w, x, mods)


def _pool_kernel(hc_ref, pw_ref, ps_ref, wo_ref, x_ref, mod_ref, o_ref, *, win, sub):
    tm, d_c = hc_ref.shape
    ng = len(POOL_WINDOWS)
    pch = d_c // ng
    assert win & (win - 1) == 0
    ri = lax.broadcasted_iota(I32, (sub, sub), 0)
    ci = lax.broadcasted_iota(I32, (sub, sub), 1)
    same = lax.shift_right_logical(ri, win.bit_length() - 1) == lax.shift_right_logical(ci, win.bit_length() - 1)
    pos = lax.broadcasted_iota(I32, (sub, 1), 0) & (win - 1)
    mixed = []
    for g, k in enumerate(POOL_WINDOWS):
        off = ci - ri
        band = jnp.where(same & (off >= -(k // 2)) & (off < k - k // 2), 1.0, 0.0).astype(BF16)
        lo = jnp.maximum(pos - k // 2, 0)
        hi = jnp.minimum(pos - k // 2 + k, win)
        cnt = (hi - lo).astype(F32)
        pooled = []
        for r in range(tm // sub):
            xb = hc_ref[r * sub:(r + 1) * sub, g * pch:(g + 1) * pch]
            pooled.append(_dot(band, xb) / cnt - xb.astype(F32))
        pg = jnp.concatenate(pooled, axis=0).astype(BF16)
        mixed.append(_dot(pg, pw_ref[g]) * ps_ref[:, g * pch:(g + 1) * pch])
    mix = jnp.concatenate(mixed, axis=1).astype(BF16)
    o_ref[...] = x_ref[...] + mod_ref[2:3, :] * _dot(mix, wo_ref[...])


def _pool_call(hc, pool_w, pool_scale, w_out, x, mods, layer, wl, seq_len, win):
    t, d = x.shape
    d_c = hc.shape[1]
    _, ng, pch, _ = pool_w.shape
    sub = max(win, 128)
    assert sub % win == 0 and seq_len % sub == 0
    tm, cid = _row_tile(t, seq_len, mods.shape[1], 512)
    assert tm % sub == 0
    return pl.pallas_call(
        functools.partial(_pool_kernel, win=win, sub=sub),
        grid=(t // tm,),
        in_specs=[
            pl.BlockSpec((tm, d_c), lambda i: (i, 0)),
            pl.BlockSpec((None, ng, pch, pch), lambda i: (wl, 0, 0, 0)),
            pl.BlockSpec((None, 1, d_c), lambda i: (wl, 0, 0)),
            pl.BlockSpec((None, d_c, d), lambda i: (wl, 0, 0)),
            pl.BlockSpec((tm, d), lambda i: (i, 0)),
            pl.BlockSpec((None, None, N_MOD, d), lambda i: (layer, cid(i), 0, 0)),
        ],
        out_specs=pl.BlockSpec((tm, d), lambda i: (i, 0)),
        out_shape=jax.ShapeDtypeStruct((t, d), F32),
        compiler_params=_params("parallel"),
        name="pool_out",
    )(hc, pool_w, pool_scale, w_out, x, mods)


def _ffn_kernel(x_ref, mod_ref, nw_ref, w1_ref, w3_ref, w2_ref, fn_ref, o_ref, h_ref, *, nf, final, dn):
    f = pl.program_id(1)
    tm, d = o_ref.shape

    def step(h, last):
        act = (_silu(_dot(h, w1_ref[...])) * _dot(h, w3_ref[...])).astype(BF16)
        if not (final and last):
            for n0 in range(0, d, dn):
                o_ref[:, n0:n0 + dn] += mod_ref[5:6, n0:n0 + dn] * _dot(act, w2_ref[:, n0:n0 + dn])
            return
        rb = min(FINAL_NORM_ROWS, tm)
        for r0 in range(0, tm, rb):
            y = o_ref[r0:r0 + rb, :] + mod_ref[5:6, :] * _dot(act[r0:r0 + rb], w2_ref[...])
            ms = jnp.mean(y * y, axis=-1, keepdims=True)
            o_ref[r0:r0 + rb, :] = y * lax.rsqrt(ms + EPS) * fn_ref[...]

    def run(first, last):
        if first:
            step(_norm_mod_rows(x_ref, nw_ref, mod_ref[3:4, :], mod_ref[4:5, :], h_ref, copy_ref=o_ref), last)
        else:
            step(h_ref[...], last)

    if nf == 1:
        run(True, True)
    elif not final:
        pl.when(f == 0)(lambda: run(True, False))
        pl.when(f > 0)(lambda: run(False, False))
    else:
        pl.when(f == 0)(lambda: run(True, False))
        pl.when(f == nf - 1)(lambda: run(False, True))
        if nf > 2:
            pl.when((f > 0) & (f < nf - 1))(lambda: run(False, False))


def _ffn_call(x, mods, norm_w, w, final_nw, layer, seq_len, final):
    t, d = x.shape
    w1, w3, w2 = w
    dff = w1.shape[1]
    tm, cid = _row_tile(t, seq_len, mods.shape[1], 1024)
    tf = _tile(dff, 512)
    nf = dff // tf
    return pl.pallas_call(
        functools.partial(_ffn_kernel, nf=nf, final=final, dn=_tile(d, 512)),
        grid=(t // tm, nf),
        in_specs=[
            pl.BlockSpec((tm, d), lambda i, f: (i, 0)),
            pl.BlockSpec((None, None, N_MOD, d), lambda i, f: (layer, cid(i), 0, 0)),
            pl.BlockSpec((None, 1, d), lambda i, f: (layer, 0, 0)),
            pl.BlockSpec((d, tf), lambda i, f: (0, f)),
            pl.BlockSpec((d, tf), lambda i, f: (0, f)),
            pl.BlockSpec((tf, d), lambda i, f: (f, 0)),
            pl.BlockSpec((1, d), lambda i, f: (0, 0)),
        ],
        out_specs=pl.BlockSpec((tm, d), lambda i, f: (i, 0)),
        out_shape=jax.ShapeDtypeStruct((t, d), F32),
        scratch_shapes=[pltpu.VMEM((tm, d), BF16)],
        compiler_params=_params("parallel", "arbitrary"),
        name="ffn",
    )(x, mods, norm_w, w1, w3, w2, final_nw)


def _mixer_layer(x, i, seq_len, mods, win, h0T, state, p, cast=None):
    converted = ()
    if i % 2 == 0:
        e = i // 2
        proj, dtT = _in_call(x, mods, p["norm_mix"], p["w_in_even"], i, e, p["n_main"], seq_len, 2304,
                             p["w_in_dtT"])
        xbc, converted = _conv_call(proj, p["ssd_conv_w"], p["ssd_conv_b"], e, seq_len, p["xbc_block"], cast)
        res = _ssd_call(xbc, dtT, p["ssd_dt_biasT"], p["ssd_a_logT"], p["ssd_d_x"], e, seq_len,
                        p["heads"], p["hdim"], p["nstate"], h0T, state)
        if state is not None:
            state = res[1]
        x = _mix_even_call(proj, res[0], p["z_block"], p["gmlp_ws"], p["gmlp_bsT"], p["gmlp_norm"],
                           p["ssd_norm"], p["w_out_even"], x, mods, i, e, seq_len)
    else:
        o = i // 2
        (hc,) = _in_call(x, mods, p["norm_mix"], p["w_in_odd"], i, o, p["w_in_odd"].shape[2], seq_len, 2048)
        x = _pool_call(hc, p["pool_w"], p["pool_scale"], p["w_out_odd"], x, mods, i, o, seq_len, win)
    return x, state, converted


def kernel(x_prompt, x_sample, state_ssd, c, c_ctx, w_mod, b_mod, norm_mix, norm_ffn, w_in_even, w_out_even, gmlp_norm, gmlp_ws, gmlp_bs, ssd_conv_w, ssd_conv_b, ssd_dt_bias, ssd_a_log, ssd_d, ssd_norm, w_in_odd, pool_w, pool_scale, w_out_odd, ffn_w1, ffn_w3, ffn_w2, final_norm):
    batch, seq, d = x_prompt.shape
    dec_batch, dec_seq, _ = x_sample.shape
    depth = w_mod.shape[0]
    n_even = w_in_even.shape[0]
    heads, hdim, nstate = state_ssd.shape[3:]
    d_b = heads * hdim
    d_a = gmlp_norm.shape[1]
    cc = ssd_conv_w.shape[2]
    n_main = 2 * d_a + d_b + cc
    h2 = 2 * heads
    assert w_in_even.shape[2] == n_main + h2 and h2 <= LANES
    assert (2 * d_a) % d_b == 0 and (n_main - cc) % cc == 0

    n_cond = 1 + dec_batch
    rows = -(-n_cond // 8) * 8
    cond = jnp.concatenate([c_ctx[None], c, jnp.zeros((rows - n_cond, d), F32)], axis=0)
    mods = _mod_call(cond, w_mod, b_mod).reshape(depth, rows, N_MOD, d)

    p = dict(
        heads=heads, hdim=hdim, nstate=nstate, n_main=n_main,
        z_block=(2 * d_a) // d_b, xbc_block=(n_main - cc) // cc,
        norm_mix=norm_mix.reshape(depth, 1, d),
        w_in_even=w_in_even.astype(BF16),
        w_in_dtT=jnp.swapaxes(w_in_even[:, :, n_main:], 1, 2).astype(BF16),
        w_out_even=w_out_even.astype(BF16),
        gmlp_norm=gmlp_norm.reshape(n_even, 1, d_a), gmlp_ws=gmlp_ws.astype(BF16),
        gmlp_bsT=jnp.swapaxes(gmlp_bs, 1, 2),
        ssd_conv_w=ssd_conv_w, ssd_conv_b=ssd_conv_b.reshape(n_even, 1, cc),
        ssd_dt_biasT=ssd_dt_bias.reshape(n_even, h2, 1), ssd_a_logT=ssd_a_log.reshape(n_even, h2, 1),
        ssd_d_x=jnp.repeat(ssd_d, hdim, axis=1).reshape(n_even, 1, d_b),
        ssd_norm=ssd_norm.reshape(n_even, 1, d_b),
        w_in_odd=w_in_odd.astype(BF16), pool_w=pool_w.astype(BF16),
        pool_scale=pool_scale.reshape(-1, 1, d), w_out_odd=w_out_odd.astype(BF16),
    )
    norm_ffn = norm_ffn.reshape(depth, 1, d)
    final_norm = final_norm.reshape(1, d)

    x_p = x_prompt.reshape(batch * seq, d)
    x_s = x_sample.reshape(dec_batch * dec_seq, d)
    mods_p, mods_s = mods[:, 0:1], mods[:, 1:n_cond]
    h0T = jnp.swapaxes(state_ssd.reshape(dec_batch, n_even, 2, d_b, nstate), 3, 4)
    state_new = jnp.zeros((batch, n_even, 2, d_b, nstate), F32)

    ffn_f32 = (ffn_w1, ffn_w3, ffn_w2)
    ffn_bf16 = {0: tuple(w[0].astype(BF16) for w in ffn_f32)}

    def can_host(t, seq_len):
        nsteps = _conv_steps(t, seq_len)[1]
        blocks = [_cast_blocks(w.shape[1], nsteps) for w in ffn_f32]
        return all(nb is not None and w.shape[1] // nb * w.shape[2] * 4 <= CAST_BLOCK_BYTES
                   for nb, w in zip(blocks, ffn_f32))

    for i in range(depth):
        pending = [j for j in range(i + 1, depth) if j not in ffn_bf16]
        cast_s = cast_p = None
        if i % 2 == 0:
            if pending and can_host(x_s.shape[0], dec_seq):
                cast_s = (ffn_f32, pending.pop(0))
            if pending and can_host(x_p.shape[0], seq):
                cast_p = (ffn_f32, pending.pop(0))
        x_p, state_new, conv_p = _mixer_layer(x_p, i, seq, mods_p, seq, None, state_new, p, cast_p)
        x_s, _, conv_s = _mixer_layer(x_s, i, dec_seq, mods_s, GRID_W, h0T, None, p, cast_s)
        if cast_p is not None:
            ffn_bf16[cast_p[1]] = conv_p
        if cast_s is not None:
            ffn_bf16[cast_s[1]] = conv_s
        if i not in ffn_bf16:
            ffn_bf16[i] = tuple(w[i].astype(BF16) for w in ffn_f32)
        last = i == depth - 1
        x_p = _ffn_call(x_p, mods_p, norm_ffn, ffn_bf16[i], final_norm, i, seq, last)
        x_s = _ffn_call(x_s, mods_s, norm_ffn, ffn_bf16[i], final_norm, i, dec_seq, last)

    return (x_p.reshape(batch, seq, d), x_s.reshape(dec_batch, dec_seq, d),
            state_new.reshape(batch, n_even, 2, heads, hdim, nstate))
```

```python
import functools
import math

import jax
import jax.numpy as jnp
from jax import lax
from jax.experimental import pallas as pl
from jax.experimental.pallas import tpu as pltpu

F32 = jnp.float32
BF16 = jnp.bfloat16
I32 = jnp.int32

EPS = 1e-6
N_MOD = 6
GRID_W = 64
CHUNK = 128
SSD_CHUNK = 128
SSD_STEP_CHUNKS = 4
SSD_GROUPS = 4
D_CONV = 5
POOL_WINDOWS = (2, 4, 8, 16)
LANES = 128
HALO = 16
NORM_ROWS = 32
FINAL_NORM_ROWS = 256
CONV_COLS = 512
CAST_BLOCK_BYTES = 2 * 1024 * 1024
LOG2E = 1.4426950408889634
VMEM_LIMIT_BYTES = 56 * 1024 * 1024


def _params(*sem):
    return pltpu.CompilerParams(dimension_semantics=sem, vmem_limit_bytes=VMEM_LIMIT_BYTES)


def _tile(n, pref):
    t = min(n, pref)
    assert n % t == 0, (n, pref)
    return t


def _row_tile(t, seq_len, n_cond, pref):
    if n_cond == 1:
        return _tile(t, pref), (lambda i: 0)
    tm = _tile(seq_len, min(pref, seq_len))
    return tm, (lambda i: (i * tm) // seq_len)


def _silu(x):
    return x * jax.nn.sigmoid(x)


def _softplus(x):
    return jnp.maximum(x, 0.0) + jnp.log1p(jnp.exp(-jnp.abs(x)))


def _split3(x):
    x1 = x.astype(BF16)
    r = x - x1.astype(F32)
    x2 = r.astype(BF16)
    x3 = (r - x2.astype(F32)).astype(BF16)
    return x1, x2, x3


def _dot(a, b):
    return jnp.dot(a, b, preferred_element_type=F32)


def _norm_mod_rows(x_ref, nw_ref, shift, scale, h_ref, copy_ref=None):
    gain = nw_ref[...] * (1.0 + scale)
    tm = x_ref.shape[0]
    rows = min(NORM_ROWS, tm)
    blocks = []
    for r in range(0, tm, rows):
        x = x_ref[r:r + rows, :]
        if copy_ref is not None:
            copy_ref[r:r + rows, :] = x
        ms = jnp.mean(x * x, axis=-1, keepdims=True)
        hb = (x * lax.rsqrt(ms + EPS) * gain + shift).astype(h_ref.dtype)
        h_ref[r:r + rows, :] = hb
        blocks.append(hb)
    return jnp.concatenate(blocks, axis=0)


def _mod_kernel(c_ref, w_ref, b_ref, o_ref):
    s = _silu(c_ref[...])
    w = w_ref[...]
    s1 = s.astype(BF16)
    s2 = (s - s1.astype(F32)).astype(BF16)
    w1 = w.astype(BF16)
    w2 = (w - w1.astype(F32)).astype(BF16)
    o_ref[...] = (_dot(s1, w1) + _dot(s2, w1) + _dot(s1, w2)) + b_ref[...]


def _mod_call(cond, w_mod, b_mod):
    depth, d, n = w_mod.shape
    rows = cond.shape[0]
    tn = _tile(n, 1024)
    return pl.pallas_call(
        _mod_kernel,
        grid=(depth, n // tn),
        in_specs=[
            pl.BlockSpec((rows, d), lambda l, j: (0, 0)),
            pl.BlockSpec((None, d, tn), lambda l, j: (l, 0, j)),
            pl.BlockSpec((None, 1, tn), lambda l, j: (l, 0, j)),
        ],
        out_specs=pl.BlockSpec((None, rows, tn), lambda l, j: (l, 0, j)),
        out_shape=jax.ShapeDtypeStruct((depth, rows, n), F32),
        compiler_params=_params("parallel", "parallel"),
        name="mod",
    )(cond, w_mod, b_mod.reshape(depth, 1, n))


def _in_kernel(x_ref, mod_ref, nw_ref, w_ref, *rest, with_dt):
    if with_dt:
        wdtT_ref, o_ref, dtT_ref, h_ref = rest
    else:
        o_ref, h_ref = rest

    j = pl.program_id(1)

    @pl.when(j == 0)
    def _():
        h = _norm_mod_rows(x_ref, nw_ref, mod_ref[0:1, :], mod_ref[1:2, :], h_ref)
        o_ref[...] = _dot(h, w_ref[...]).astype(o_ref.dtype)
        if with_dt:
            dtT_ref[...] = lax.dot_general(wdtT_ref[...], h, (((1,), (1,)), ((), ())),
                                           preferred_element_type=F32)

    @pl.when(j > 0)
    def _():
        o_ref[...] = _dot(h_ref[...], w_ref[...]).astype(o_ref.dtype)


def _in_call(x, mods, norm_w, w, layer, wl, n, seq_len, tn_pref, wdtT=None):
    t, d = x.shape
    tm, cid = _row_tile(t, seq_len, mods.shape[1], 1024)
    tn = _tile(n, tn_pref)
    with_dt = wdtT is not None
    in_specs = [
        pl.BlockSpec((tm, d), lambda i, j: (i, 0)),
        pl.BlockSpec((None, None, N_MOD, d), lambda i, j: (layer, cid(i), 0, 0)),
        pl.BlockSpec((None, 1, d), lambda i, j: (layer, 0, 0)),
        pl.BlockSpec((None, d, tn), lambda i, j: (wl, 0, j)),
    ]
    out_specs = [pl.BlockSpec((tm, tn), lambda i, j: (i, j))]
    out_shape = [jax.ShapeDtypeStruct((t, n), BF16)]
    args = [x, mods, norm_w, w]
    if with_dt:
        h2 = wdtT.shape[1]
        in_specs.append(pl.BlockSpec((None, h2, d), lambda i, j: (wl, 0, 0)))
        out_specs.append(pl.BlockSpec((h2, tm), lambda i, j: (0, i)))
        out_shape.append(jax.ShapeDtypeStruct((h2, t), F32))
        args.append(wdtT)
    return pl.pallas_call(
        functools.partial(_in_kernel, with_dt=with_dt),
        grid=(t // tm, n // tn),
        in_specs=in_specs,
        out_specs=out_specs,
        out_shape=out_shape,
        scratch_shapes=[pltpu.VMEM((tm, d), BF16)],
        compiler_params=_params("parallel", "arbitrary"),
        name="in_proj",
    )(*args)


def _gelu_tanh(x):
    k = math.sqrt(2.0 / math.pi)
    a = -2.0 * k * 0.044715 * LOG2E
    b = -2.0 * k * LOG2E
    return x / (1.0 + jnp.exp2(x * (x * x * a + b)))


def _cast_blocks(rows, nsteps):
    for nb in range(min(nsteps, rows), 0, -1):
        if rows % nb == 0 and (rows // nb) % 16 == 0:
            return nb
    return None


def _can_cast(casts, nsteps):
    for w, _ in casts:
        nb = _cast_blocks(w.shape[1], nsteps)
        if nb is None or w.shape[1] // nb * w.shape[2] * 4 > CAST_BLOCK_BYTES:
            return False
    return True


def _cast_plan(casts, nsteps, step_of):
    in_specs, out_specs, out_shape, args = [], [], [], []
    for w, wl in casts:
        rows, cols = w.shape[1:]
        nb = _cast_blocks(rows, nsteps)

        def blk(*g, nb=nb):
            return jnp.minimum(step_of(*g), nb - 1)

        in_specs.append(pl.BlockSpec((None, rows // nb, cols), lambda *g, blk=blk, wl=wl: (wl, blk(*g), 0)))
        out_specs.append(pl.BlockSpec((rows // nb, cols), lambda *g, blk=blk: (blk(*g), 0)))
        out_shape.append(jax.ShapeDtypeStruct((rows, cols), BF16))
        args.append(w)
    return in_specs, out_specs, out_shape, args


def _conv_kernel(cur_ref, prev_ref, next_ref, w_ref, b_ref, *rest, nc, n_cast):
    cast_in, o_ref, cast_out = rest[:n_cast], rest[n_cast], rest[n_cast + 1:]
    for src, dst in zip(cast_in, cast_out):
        dst[...] = src[...].astype(dst.dtype)

    c = pl.program_id(1)
    bq, cc = cur_ref.shape
    q = min(bq, SSD_CHUNK)
    rows = q + 2 * HALO
    ri = lax.broadcasted_iota(I32, (q, rows), 0)
    ci = lax.broadcasted_iota(I32, (q, rows), 1)
    picks = [None if k == D_CONV // 2 else
             jnp.where(ci == ri + (HALO + k - D_CONV // 2), 1.0, 0.0).astype(BF16) for k in range(D_CONV)]
    cw = min(CONV_COLS, cc)
    nsub = bq // q
    for n0 in range(0, cc, cw):
        cs = slice(n0, n0 + cw)
        prev = jnp.where(c > 0, prev_ref[:, cs], jnp.zeros((HALO, cw), BF16))
        nxt = jnp.where(c < nc - 1, next_ref[:, cs], jnp.zeros((HALO, cw), BF16))
        for u in range(nsub):
            cur = cur_ref[u * q:(u + 1) * q, cs]
            before = prev if u == 0 else cur_ref[u * q - HALO:u * q, cs]
            after = nxt if u == nsub - 1 else cur_ref[(u + 1) * q:(u + 1) * q + HALO, cs]
            ext = jnp.concatenate([before, cur, after], axis=0)
            acc = b_ref[:, cs]
            for k in range(D_CONV):
                tap = cur.astype(F32) if picks[k] is None else _dot(picks[k], ext)
                acc = acc + w_ref[k:k + 1, cs] * tap
            o_ref[u * q:(u + 1) * q, cs] = _silu(acc).astype(o_ref.dtype)


def _conv_steps(t, seq_len):
    q = _tile(seq_len, 2 * SSD_CHUNK)
    return q, (t // seq_len) * (seq_len // q)


def _conv_call(proj, conv_w, conv_b, layer, seq_len, col_block, cast=None):
    t = proj.shape[0]
    cc = conv_w.shape[2]
    q, nsteps = _conv_steps(t, seq_len)
    nc = seq_len // q
    nseq = t // seq_len
    per = q // HALO
    last = t // HALO - 1
    in_specs = [
        pl.BlockSpec((q, cc), lambda s, c: (s * nc + c, col_block)),
        pl.BlockSpec((HALO, cc), lambda s, c: (jnp.maximum((s * nc + c) * per - 1, 0), col_block)),
        pl.BlockSpec((HALO, cc), lambda s, c: (jnp.minimum((s * nc + c + 1) * per, last), col_block)),
        pl.BlockSpec((None, D_CONV, cc), lambda s, c: (layer, 0, 0)),
        pl.BlockSpec((None, 1, cc), lambda s, c: (layer, 0, 0)),
    ]
    args = [proj, proj, proj, conv_w, conv_b]
    out_specs = [pl.BlockSpec((q, cc), lambda s, c: (s * nc + c, 0))]
    out_shape = [jax.ShapeDtypeStruct((t, cc), BF16)]
    cast = cast or []
    extra = _cast_plan(cast, nsteps, lambda s, c: s * nc + c)
    in_specs += extra[0]
    out_specs += extra[1]
    out_shape += extra[2]
    args += extra[3]
    n_cast = len(cast)
    res = pl.pallas_call(
        functools.partial(_conv_kernel, nc=nc, n_cast=n_cast),
        grid=(nseq, nc),
        in_specs=in_specs,
        out_specs=out_specs,
        out_shape=out_shape,
        compiler_params=_params("arbitrary", "arbitrary"),
        name="conv",
    )(*args)
    return res[0], tuple(res[1:])


def _ssd_factors(dtT_ref, lanes, dtbT_ref, alogT_ref, rev, heads):
    q = lanes.stop - lanes.start

    def dsel(fwd, bwd):
        return jnp.where(rev, bwd, fwd)

    dtT = _softplus(dsel(dtT_ref[0:heads, lanes], dtT_ref[heads:2 * heads, lanes])
                    + dsel(dtbT_ref[0:heads, :], dtbT_ref[heads:2 * heads, :]))
    aT = dtT * (-jnp.exp(dsel(alogT_ref[0:heads, :], alogT_ref[heads:2 * heads, :]))) * LOG2E
    ri = lax.broadcasted_iota(I32, (q, q), 0)
    ci = lax.broadcasted_iota(I32, (q, q), 1)
    triT = dsel(jnp.where(ci >= ri, 1.0, 0.0), jnp.where(ri >= ci, 1.0, 0.0)).astype(BF16)
    b1, b2, b3 = _split3(aT)
    a_csT = (_dot(b1, triT) + _dot(b2, triT)) + _dot(b3, triT)
    a_cs = a_csT.T
    tot = dsel(a_cs[q - 1:q, :], a_cs[0:1, :])
    stack = jnp.concatenate([dtT.T, jnp.exp2(a_cs), jnp.exp2(tot - a_cs)], axis=0)
    s_hi = stack.astype(BF16)
    s_lo = (stack - s_hi.astype(F32)).astype(BF16)
    return a_cs, a_csT, jnp.concatenate([s_hi, s_lo], axis=1)


def _ssd_kernel(xbc_ref, dtT_ref, dtTn_ref, dtbT_ref, alogT_ref, dsk_ref,
                *rest, nc, heads, hdim, groups, nstate, has_h0, want_state, n_cast):
    rest = list(rest)
    h0_ref = so_ref = None
    if has_h0:
        h0_ref = rest.pop(0)
    cast_in = [rest.pop(0) for _ in range(n_cast)]
    y_ref = rest.pop(0)
    if want_state:
        so_ref = rest.pop(0)
    cast_out = [rest.pop(0) for _ in range(n_cast)]
    st_ref, yf_ref, acs_ref, acsT_ref, scat_ref = rest

    for src, dst in zip(cast_in, cast_out):
        dst[...] = src[...].astype(dst.dtype)

    q = SSD_CHUNK
    bq = xbc_ref.shape[0]
    nsub = bq // q
    db = heads * hdim
    gn = groups * nstate
    hpg = heads // groups
    gw = hpg * hdim
    seq_len = nc * bq
    t = pl.program_id(1)
    rev = t >= nc
    c = jnp.where(rev, 2 * nc - 1 - t, t)
    slot = t & 1

    def dsel(fwd, bwd):
        return jnp.where(rev, bwd, fwd)

    def stage(src_ref, is_rev, dst):
        for s in range(nsub):
            a_cs, a_csT, s_cat = _ssd_factors(src_ref, slice(s * q, (s + 1) * q), dtbT_ref, alogT_ref, is_rev,
                                              heads)
            acs_ref[dst, s] = a_cs
            acsT_ref[dst, s] = a_csT
            scat_ref[dst, s] = s_cat

    @pl.when(t == 0)
    def _():
        stage(dtT_ref, rev, slot)

    @pl.when((t == 0) | (t == nc))
    def _():
        st_ref[...] = h0_ref[...] if has_h0 else jnp.zeros_like(st_ref)

    ri = lax.broadcasted_iota(I32, (q, q), 0)
    ci = lax.broadcasted_iota(I32, (q, q), 1)
    mask = dsel(jnp.where(ri >= ci, 1.0, 0.0), jnp.where(ci >= ri, 1.0, 0.0)) > 0.5

    assert hdim & (hdim - 1) == 0 and heads & (heads - 1) == 0
    eh = lax.broadcasted_iota(I32, (2 * heads, db), 0) & (heads - 1)
    ec = lax.shift_right_logical(lax.broadcasted_iota(I32, (2 * heads, db), 1), hdim.bit_length() - 1)
    expand = jnp.where(eh == ec, 1.0, 0.0).astype(BF16)
    low_half = (lax.broadcasted_iota(I32, (q, gw), 1) & hdim) == 0

    for u in range(nsub):
        su = jnp.where(rev, nsub - 1 - u, u) if nsub > 1 else 0
        rows = pl.ds(pl.multiple_of(su * q, q), q) if nsub > 1 else slice(None)
        row0 = pl.multiple_of(c * bq + su * q, q)
        dst0 = pl.multiple_of(jnp.where(rev, seq_len, row0), q)
        a_cs = acs_ref[slot, su]
        a_csT = acsT_ref[slot, su]
        s_cat = scat_ref[slot, su]

        for g in range(groups):
            gc = slice(g * gw, (g + 1) * gw)
            ex = _dot(s_cat, expand[:, gc])
            dt_x, e_x, dec_x = ex[0:q], ex[q:2 * q], ex[2 * q:3 * q]
            xg = xbc_ref[rows, gc].astype(F32) * dt_x
            xg_lo = jnp.where(low_half, xg, 0.0).astype(BF16)
            xg_hi = jnp.where(low_half, 0.0, xg).astype(BF16)
            xdb = (xg * dec_x).astype(BF16)
            bg = xbc_ref[rows, db + g * nstate:db + (g + 1) * nstate]
            cg = xbc_ref[rows, db + gn + g * nstate:db + gn + (g + 1) * nstate]
            cb = lax.dot_general(cg, bg, (((1,), (1,)), ((), ())), preferred_element_type=F32)
            yo = _dot(cg, st_ref[:, gc].astype(BF16)) * e_x
            for pr in range(hpg // 2):
                h0 = g * hpg + 2 * pr
                ls = []
                for h in (h0, h0 + 1):
                    d = a_cs[:, h:h + 1] - a_csT[h:h + 1, :]
                    ls.append(jnp.where(mask, jnp.exp2(d), 0.0) * cb)
                lhs = jnp.concatenate(ls, axis=1).astype(BF16)
                pc = slice(2 * pr * hdim, (2 * pr + 2) * hdim)
                rhs = jnp.concatenate([xg_lo[:, pc], xg_hi[:, pc]], axis=0)
                yp = _dot(lhs, rhs) + yo[:, pc]
                cols = slice(h0 * hdim, (h0 + 2) * hdim)
                yf_ref[pl.ds(dst0, q), cols] = yp.astype(yf_ref.dtype)
                y_ref[rows, cols] = ((yf_ref[pl.ds(row0, q), cols].astype(F32) + yp)
                                     + dsk_ref[:, cols] * xbc_ref[rows, cols].astype(F32)).astype(y_ref.dtype)
            new_state = lax.dot_general(bg, xdb, (((0,), (0,)), ((), ())), preferred_element_type=F32)
            st_ref[:, gc] = st_ref[:, gc] * dsel(e_x[q - 1:q, :], e_x[0:1, :]) + new_state

    stage(dtTn_ref, t + 1 >= nc, 1 - slot)

    if want_state:
        @pl.when((t == nc - 1) | (t == 2 * nc - 1))
        def _():
            so_ref[...] = st_ref[...].T


def _ssd_steps(t, seq_len):
    bq = _tile(seq_len, SSD_STEP_CHUNKS * SSD_CHUNK)
    return bq, (t // seq_len) * 2 * (seq_len // bq)


def _ssd_call(xbc, dtT, dt_biasT, a_logT, d_skip_x, layer, seq_len, heads, hdim, nstate, h0T=None, state_out=None,
              cast=None):
    t, cc = xbc.shape
    q = SSD_CHUNK
    bq, nsteps = _ssd_steps(t, seq_len)
    nsub = bq // q
    nc = seq_len // bq
    nseq = t // seq_len
    db = heads * hdim
    h2 = 2 * heads
    has_h0 = h0T is not None
    want_state = state_out is not None
    n_even = dt_biasT.shape[0]

    def chunk(tt):
        return jnp.where(tt >= nc, 2 * nc - 1 - tt, tt)

    def late(tt):
        return jnp.where(tt >= nc, 2 * nc - 1 - tt, nc - 1)

    def direction(tt):
        return jnp.where(tt >= nc, 1, 0)

    in_specs = [
        pl.BlockSpec((bq, cc), lambda s, tt: (s * nc + chunk(tt), 0)),
        pl.BlockSpec((h2, bq), lambda s, tt: (0, s * nc + chunk(tt))),
        pl.BlockSpec((h2, bq), lambda s, tt: (0, s * nc + chunk(jnp.minimum(tt + 1, 2 * nc - 1)))),
        pl.BlockSpec((None, h2, 1), lambda s, tt: (layer, 0, 0)),
        pl.BlockSpec((None, h2, 1), lambda s, tt: (layer, 0, 0)),
        pl.BlockSpec((None, 1, db), lambda s, tt: (layer, 0, 0)),
    ]
    args = [xbc, dtT, dtT, dt_biasT, a_logT, d_skip_x]
    if has_h0:
        in_specs.append(pl.BlockSpec((None, None, None, nstate, db),
                                     lambda s, tt: (s, layer, direction(tt), 0, 0)))
        args.append(h0T)
    cast = cast or []
    extra = _cast_plan(cast, nsteps, lambda s, tt: s * (2 * nc) + tt)
    in_specs += extra[0]
    args += extra[3]
    out_specs = [pl.BlockSpec((bq, db), lambda s, tt: (s * nc + late(tt), 0))]
    out_shape = [jax.ShapeDtypeStruct((t, db), BF16)]
    aliases = {}
    if want_state:
        out_specs.append(pl.BlockSpec((None, None, None, db, nstate),
                                      lambda s, tt: (s, layer, direction(tt), 0, 0)))
        out_shape.append(jax.ShapeDtypeStruct((nseq, n_even, 2, db, nstate), F32))
        in_specs.append(pl.BlockSpec(memory_space=pl.ANY))
        args.append(state_out)
        aliases = {len(args) - 1: 1}
    out_specs += extra[1]
    out_shape += extra[2]
    kern = functools.partial(_ssd_kernel, nc=nc, heads=heads, hdim=hdim, groups=SSD_GROUPS, nstate=nstate,
                             has_h0=has_h0, want_state=want_state, n_cast=len(cast))
    if aliases:
        inner = kern
        n_in = len(args)

        def kern(*refs):
            return inner(*refs[:n_in - 1], *refs[n_in:])

    return pl.pallas_call(
        kern,
        grid=(nseq, 2 * nc),
        in_specs=in_specs,
        out_specs=out_specs,
        out_shape=out_shape,
        input_output_aliases=aliases,
        scratch_shapes=[
            pltpu.VMEM((nstate, db), F32),
            pltpu.VMEM((seq_len + q, db), BF16),
            pltpu.VMEM((2, nsub, q, heads), F32),
            pltpu.VMEM((2, nsub, heads, q), F32),
            pltpu.VMEM((2, nsub, 3 * q, h2), BF16),
        ],
        compiler_params=_params("arbitrary", "arbitrary"),
        name="ssd",
    )(*args)


def _mix_even_kernel(u_ref, v_ref, z_ref, ys_ref, ws_ref, bsT_ref, vn_ref, nw_ref, w_ref, x_ref, mod_ref, o_ref,
                     b_ref, *, groups):
    tm, d_a = u_ref.shape
    d_b = ys_ref.shape[1]
    ch = d_a // groups
    assert d_a == d_b
    gz = ys_ref[...].astype(F32) * _silu(z_ref[...].astype(F32))
    ms = jnp.mean(gz * gz, axis=-1, keepdims=True)
    b_ref[...] = (gz * lax.rsqrt(ms + EPS) * nw_ref[...]).astype(BF16)
    acc = None
    for g in range(groups):
        sl = slice(g * ch, (g + 1) * ch)
        part = _dot(b_ref[:, sl], w_ref[d_a + g * ch:d_a + (g + 1) * ch, :])
        acc = part if acc is None else acc + part
        vns = []
        for c0 in range(0, tm, CHUNK):
            v = _gelu_tanh(v_ref[c0:c0 + CHUNK, sl].astype(F32))
            ms = jnp.mean(v * v, axis=-1, keepdims=True)
            vns.append((v * lax.rsqrt(ms + EPS) * vn_ref[:, sl]).astype(BF16))
        sv = _dot(ws_ref[g], jnp.concatenate(vns, axis=1)) + bsT_ref[:, g:g + 1]
        parts = [(_gelu_tanh(u_ref[c * CHUNK:(c + 1) * CHUNK, sl].astype(F32)) * sv[:, c * ch:(c + 1) * ch])
                 .astype(BF16) for c in range(tm // CHUNK)]
        acc = acc + _dot(jnp.concatenate(parts, axis=0), w_ref[sl, :])
    o_ref[...] = x_ref[...] + mod_ref[2:3, :] * acc


def _mix_even_call(proj, ysum, z_block, ws, bsT, vnorm, norm_w, w, x, mods, layer, wl, w_idx, seq_len):
    t, d = x.shape
    d_b = ysum.shape[1]
    d_a = w.shape[1] - d_b
    groups = ws.shape[1]
    assert d_a == d_b
    tm, cid = _row_tile(t, seq_len, mods.shape[1], 2 * CHUNK)
    return pl.pallas_call(
        functools.partial(_mix_even_kernel, groups=groups),
        grid=(t // tm,),
        in_specs=[
            pl.BlockSpec((tm, d_a), lambda i: (i, 0)),
            pl.BlockSpec((tm, d_a), lambda i: (i, 1)),
            pl.BlockSpec((tm, d_b), lambda i: (i, z_block)),
            pl.BlockSpec((tm, d_b), lambda i: (i, 0)),
            pl.BlockSpec((None, groups, CHUNK, CHUNK), lambda i: (wl, 0, 0, 0)),
            pl.BlockSpec((None, CHUNK, groups), lambda i: (wl, 0, 0)),
            pl.BlockSpec((None, 1, d_a), lambda i: (wl, 0, 0)),
            pl.BlockSpec((None, 1, d_b), lambda i: (wl, 0, 0)),
            pl.BlockSpec((None, d_a + d_b, d), lambda i: (w_idx, 0, 0), pipeline_mode=pl.Buffered(1)),
            pl.BlockSpec((tm, d), lambda i: (i, 0)),
            pl.BlockSpec((None, None, N_MOD, d), lambda i: (layer, cid(i), 0, 0)),
        ],
        out_specs=pl.BlockSpec((tm, d), lambda i: (i, 0)),
        out_shape=jax.ShapeDtypeStruct((t, d), F32),
        scratch_shapes=[pltpu.VMEM((tm, d_b), BF16)],
        compiler_params=_params("parallel"),
        name="mix_even",
    )(proj, proj, proj, ysum, ws, bsT, vnorm, norm_w, w, x, mods)


def _pool_kernel(hc_ref, pw_ref, ps_ref, wo_ref, x_ref, mod_ref, o_ref, *, win, sub):
    tm, d_c = hc_ref.shape
    ng = len(POOL_WINDOWS)
    pch = d_c // ng
    assert win & (win - 1) == 0
    ri = lax.broadcasted_iota(I32, (sub, sub), 0)
    ci = lax.broadcasted_iota(I32, (sub, sub), 1)
    same = lax.shift_right_logical(ri, win.bit_length() - 1) == lax.shift_right_logical(ci, win.bit_length() - 1)
    pos = lax.broadcasted_iota(I32, (sub, 1), 0) & (win - 1)
    mixed = []
    for g, k in enumerate(POOL_WINDOWS):
        off = ci - ri
        band = jnp.where(same & (off >= -(k // 2)) & (off < k - k // 2), 1.0, 0.0).astype(BF16)
        lo = jnp.maximum(pos - k // 2, 0)
        hi = jnp.minimum(pos - k // 2 + k, win)
        cnt = (hi - lo).astype(F32)
        pooled = []
        for r in range(tm // sub):
            xb = hc_ref[r * sub:(r + 1) * sub, g * pch:(g + 1) * pch]
            pooled.append(_dot(band, xb) / cnt - xb.astype(F32))
        pg = jnp.concatenate(pooled, axis=0).astype(BF16)
        mixed.append(_dot(pg, pw_ref[g]) * ps_ref[:, g * pch:(g + 1) * pch])
    mix = jnp.concatenate(mixed, axis=1).astype(BF16)
    o_ref[...] = x_ref[...] + mod_ref[2:3, :] * _dot(mix, wo_ref[...])


def _pool_call(hc, pool_w, pool_scale, w_out, x, mods, layer, wl, seq_len, win):
    t, d = x.shape
    d_c = hc.shape[1]
    _, ng, pch, _ = pool_w.shape
    sub = max(win, 128)
    assert sub % win == 0 and seq_len % sub == 0
    tm, cid = _row_tile(t, seq_len, mods.shape[1], 512)
    assert tm % sub == 0
    return pl.pallas_call(
        functools.partial(_pool_kernel, win=win, sub=sub),
        grid=(t // tm,),
        in_specs=[
            pl.BlockSpec((tm, d_c), lambda i: (i, 0)),
            pl.BlockSpec((None, ng, pch, pch), lambda i: (wl, 0, 0, 0)),
            pl.BlockSpec((None, 1, d_c), lambda i: (wl, 0, 0)),
            pl.BlockSpec((None, d_c, d), lambda i: (wl, 0, 0)),
            pl.BlockSpec((tm, d), lambda i: (i, 0)),
            pl.BlockSpec((None, None, N_MOD, d), lambda i: (layer, cid(i), 0, 0)),
        ],
        out_specs=pl.BlockSpec((tm, d), lambda i: (i, 0)),
        out_shape=jax.ShapeDtypeStruct((t, d), F32),
        compiler_params=_params("parallel"),
        name="pool_out",
    )(hc, pool_w, pool_scale, w_out, x, mods)


def _ffn_kernel(x_ref, mod_ref, nw_ref, w1_ref, w3_ref, w2_ref, fn_ref, o_ref, h_ref, *, nf, final, dn):
    f = pl.program_id(1)
    tm, d = o_ref.shape

    def step(h, last):
        act = (_silu(_dot(h, w1_ref[...])) * _dot(h, w3_ref[...])).astype(BF16)
        if not (final and last):
            for n0 in range(0, d, dn):
                o_ref[:, n0:n0 + dn] += mod_ref[5:6, n0:n0 + dn] * _dot(act, w2_ref[:, n0:n0 + dn])
            return
        rb = min(FINAL_NORM_ROWS, tm)
        for r0 in range(0, tm, rb):
            y = o_ref[r0:r0 + rb, :] + mod_ref[5:6, :] * _dot(act[r0:r0 + rb], w2_ref[...])
            ms = jnp.mean(y * y, axis=-1, keepdims=True)
            o_ref[r0:r0 + rb, :] = y * lax.rsqrt(ms + EPS) * fn_ref[...]

    def run(first, last):
        if first:
            step(_norm_mod_rows(x_ref, nw_ref, mod_ref[3:4, :], mod_ref[4:5, :], h_ref, copy_ref=o_ref), last)
        else:
            step(h_ref[...], last)

    if nf == 1:
        run(True, True)
    elif not final:
        pl.when(f == 0)(lambda: run(True, False))
        pl.when(f > 0)(lambda: run(False, False))
    else:
        pl.when(f == 0)(lambda: run(True, False))
        pl.when(f == nf - 1)(lambda: run(False, True))
        if nf > 2:
            pl.when((f > 0) & (f < nf - 1))(lambda: run(False, False))


def _ffn_call(x, mods, norm_w, w, final_nw, layer, seq_len, final):
    t, d = x.shape
    w1, w3, w2 = w
    dff = w1.shape[1]
    tm, cid = _row_tile(t, seq_len, mods.shape[1], 1024)
    tf = _tile(dff, 512)
    nf = dff // tf
    return pl.pallas_call(
        functools.partial(_ffn_kernel, nf=nf, final=final, dn=_tile(d, 512)),
        grid=(t // tm, nf),
        in_specs=[
            pl.BlockSpec((tm, d), lambda i, f: (i, 0)),
            pl.BlockSpec((None, None, N_MOD, d), lambda i, f: (layer, cid(i), 0, 0)),
            pl.BlockSpec((None, 1, d), lambda i, f: (layer, 0, 0)),
            pl.BlockSpec((d, tf), lambda i, f: (0, f)),
            pl.BlockSpec((d, tf), lambda i, f: (0, f)),
            pl.BlockSpec((tf, d), lambda i, f: (f, 0)),
            pl.BlockSpec((1, d), lambda i, f: (0, 0)),
        ],
        out_specs=pl.BlockSpec((tm, d), lambda i, f: (i, 0)),
        out_shape=jax.ShapeDtypeStruct((t, d), F32),
        scratch_shapes=[pltpu.VMEM((tm, d), BF16)],
        compiler_params=_params("parallel", "arbitrary"),
        name="ffn",
    )(x, mods, norm_w, w1, w3, w2, final_nw)


def _mixer_layer(x, i, seq_len, mods, win, h0T, state, p, cast_conv=None, cast_ssd=None):
    by_conv = by_ssd = ()
    if i % 2 == 0:
        e = i // 2
        w_out = p["w_out_even"][e]
        proj, dtT = _in_call(x, mods, p["norm_mix"], p["w_in_even"], i, e, p["n_main"], seq_len, 2304,
                             p["w_in_dtT"])
        xbc, by_conv = _conv_call(proj, p["ssd_conv_w"], p["ssd_conv_b"], e, seq_len, p["xbc_block"], cast_conv)
        res = _ssd_call(xbc, dtT, p["ssd_dt_biasT"], p["ssd_a_logT"], p["ssd_d_x"], e, seq_len,
                        p["heads"], p["hdim"], p["nstate"], h0T, state, cast_ssd)
        if state is not None:
            state = res[1]
        by_ssd = tuple(res[1 if state is None else 2:])
        x = _mix_even_call(proj, res[0], p["z_block"], p["gmlp_ws"], p["gmlp_bsT"], p["gmlp_norm"],
                           p["ssd_norm"], w_out, x, mods, i, e, 0, seq_len)
    else:
        o = i // 2
        (hc,) = _in_call(x, mods, p["norm_mix"], p["w_in_odd"], i, o, p["w_in_odd"].shape[2], seq_len, 2048)
        x = _pool_call(hc, p["pool_w"], p["pool_scale"], p["w_out_odd"], x, mods, i, o, seq_len, win)
    return x, state, by_conv, by_ssd


def kernel(x_prompt, x_sample, state_ssd, c, c_ctx, w_mod, b_mod, norm_mix, norm_ffn, w_in_even, w_out_even, gmlp_norm, gmlp_ws, gmlp_bs, ssd_conv_w, ssd_conv_b, ssd_dt_bias, ssd_a_log, ssd_d, ssd_norm, w_in_odd, pool_w, pool_scale, w_out_odd, ffn_w1, ffn_w3, ffn_w2, final_norm):
    batch, seq, d = x_prompt.shape
    dec_batch, dec_seq, _ = x_sample.shape
    depth = w_mod.shape[0]
    n_even = w_in_even.shape[0]
    heads, hdim, nstate = state_ssd.shape[3:]
    d_b = heads * hdim
    d_a = gmlp_norm.shape[1]
    cc = ssd_conv_w.shape[2]
    n_main = 2 * d_a + d_b + cc
    h2 = 2 * heads
    assert w_in_even.shape[2] == n_main + h2 and h2 <= LANES
    assert (2 * d_a) % d_b == 0 and (n_main - cc) % cc == 0

    n_cond = 1 + dec_batch
    rows = -(-n_cond // 8) * 8
    cond = jnp.concatenate([c_ctx[None], c, jnp.zeros((rows - n_cond, d), F32)], axis=0)
    mods = _mod_call(cond, w_mod, b_mod).reshape(depth, rows, N_MOD, d)

    p = dict(
        heads=heads, hdim=hdim, nstate=nstate, n_main=n_main,
        z_block=(2 * d_a) // d_b, xbc_block=(n_main - cc) // cc,
        norm_mix=norm_mix.reshape(depth, 1, d),
        w_in_even=w_in_even.astype(BF16),
        w_in_dtT=jnp.swapaxes(lax.optimization_barrier(w_in_even[:, :, n_main:]), 1, 2).astype(BF16),
        gmlp_norm=gmlp_norm.reshape(n_even, 1, d_a), gmlp_ws=gmlp_ws.astype(BF16),
        gmlp_bsT=jnp.swapaxes(gmlp_bs, 1, 2),
        ssd_conv_w=ssd_conv_w, ssd_conv_b=ssd_conv_b.reshape(n_even, 1, cc),
        ssd_dt_biasT=ssd_dt_bias.reshape(n_even, h2, 1), ssd_a_logT=ssd_a_log.reshape(n_even, h2, 1),
        ssd_d_x=jnp.repeat(ssd_d, hdim, axis=1).reshape(n_even, 1, d_b),
        ssd_norm=ssd_norm.reshape(n_even, 1, d_b),
        pool_scale=pool_scale.reshape(-1, 1, d),
    )
    norm_ffn = norm_ffn.reshape(depth, 1, d)
    final_norm = final_norm.reshape(1, d)

    x_p = x_prompt.reshape(batch * seq, d)
    x_s = x_sample.reshape(dec_batch * dec_seq, d)
    mods_p, mods_s = mods[:, 0:1], mods[:, 1:n_cond]
    h0T = jnp.swapaxes(state_ssd.reshape(dec_batch, n_even, 2, d_b, nstate), 3, 4)
    state_new = jnp.zeros((batch, n_even, 2, d_b, nstate), F32)

    ffn_f32 = (ffn_w1, ffn_w3, ffn_w2)
    ffn_bf16 = {0: tuple(w[0].astype(BF16) for w in ffn_f32)}
    p["w_out_even"] = {0: w_out_even[0:1].astype(BF16)}
    odd_f32 = (w_in_odd, w_out_odd, pool_w)
    late = [(w.reshape(1, -1, w.shape[-1]), 0) for w in odd_f32]
    late += [(w_out_even, e) for e in range(1, n_even)]
    host_late = depth > 1 and _can_cast(late, _ssd_steps(x_s.shape[0], dec_seq)[1])
    if not host_late:
        p["w_in_odd"], p["w_out_odd"], p["pool_w"] = (w.astype(BF16) for w in odd_f32)
        for e in range(1, n_even):
            p["w_out_even"][e] = w_out_even[e:e + 1].astype(BF16)

    for i in range(depth):
        pending = [j for j in range(i + 1, depth) if j not in ffn_bf16]
        cast_s = cast_p = None
        if i % 2 == 0:
            if pending and _can_cast([(w, 0) for w in ffn_f32], _conv_steps(x_s.shape[0], dec_seq)[1]):
                nxt = pending.pop(0)
                cast_s = [(w, nxt) for w in ffn_f32]
            if pending and _can_cast([(w, 0) for w in ffn_f32], _conv_steps(x_p.shape[0], seq)[1]):
                nxt = pending.pop(0)
                cast_p = [(w, nxt) for w in ffn_f32]
        x_p, state_new, conv_p, _ = _mixer_layer(x_p, i, seq, mods_p, seq, None, state_new, p, cast_p)
        x_s, _, conv_s, ssd_s = _mixer_layer(x_s, i, dec_seq, mods_s, GRID_W, h0T, None, p, cast_s,
                                             late if host_late and i == 0 else None)
        if cast_p is not None:
            ffn_bf16[cast_p[0][1]] = conv_p
        if cast_s is not None:
            ffn_bf16[cast_s[0][1]] = conv_s
        if host_late and i == 0:
            p["w_in_odd"], p["w_out_odd"], p["pool_w"] = (c.reshape(w.shape) for c, w in zip(ssd_s, odd_f32))
            for e in range(1, n_even):
                p["w_out_even"][e] = ssd_s[len(odd_f32) + e - 1][None]
        if i not in ffn_bf16:
            ffn_bf16[i] = tuple(w[i].astype(BF16) for w in ffn_f32)
        last = i == depth - 1
        x_p = _ffn_call(x_p, mods_p, norm_ffn, ffn_bf16[i], final_norm, i, seq, last)
        x_s = _ffn_call(x_s, mods_s, norm_ffn, ffn_bf16[i], final_norm, i, dec_seq, last)

    return (x_p.reshape(batch, seq, d), x_s.reshape(dec_batch, dec_seq, d),
            state_new.reshape(batch, n_even, 2, heads, hdim, nstate))
```

```python
import functools
import math

import jax
import jax.numpy as jnp
from jax import lax
from jax.experimental import pallas as pl
from jax.experimental.pallas import tpu as pltpu

F32 = jnp.float32
BF16 = jnp.bfloat16
I32 = jnp.int32

EPS = 1e-6
N_MOD = 6
GRID_W = 64
CHUNK = 128
SSD_CHUNK = 128
SSD_STEP_CHUNKS = 4
SSD_GROUPS = 4
D_CONV = 5
POOL_WINDOWS = (2, 4, 8, 16)
LANES = 128
HALO = 16
NORM_ROWS = 32
FINAL_NORM_ROWS = 256
CONV_COLS = 512
CAST_BLOCK_BYTES = 2 * 1024 * 1024
LOG2E = 1.4426950408889634
VMEM_LIMIT_BYTES = 56 * 1024 * 1024


def _params(*sem):
    return pltpu.CompilerParams(dimension_semantics=sem, vmem_limit_bytes=VMEM_LIMIT_BYTES)


def _tile(n, pref):
    t = min(n, pref)
    assert n % t == 0, (n, pref)
    return t


def _row_tile(t, seq_len, n_cond, pref):
    if n_cond == 1:
        return _tile(t, pref), (lambda i: 0)
    tm = _tile(seq_len, min(pref, seq_len))
    return tm, (lambda i: (i * tm) // seq_len)


def _silu(x):
    return x * jax.nn.sigmoid(x)


def _softplus(x):
    return jnp.maximum(x, 0.0) + jnp.log1p(jnp.exp(-jnp.abs(x)))


def _split3(x):
    x1 = x.astype(BF16)
    r = x - x1.astype(F32)
    x2 = r.astype(BF16)
    x3 = (r - x2.astype(F32)).astype(BF16)
    return x1, x2, x3


def _dot(a, b):
    return jnp.dot(a, b, preferred_element_type=F32)


def _norm_mod_rows(x_ref, nw_ref, shift, scale, h_ref, copy_ref=None):
    gain = nw_ref[...] * (1.0 + scale)
    tm = x_ref.shape[0]
    rows = min(NORM_ROWS, tm)
    blocks = []
    for r in range(0, tm, rows):
        x = x_ref[r:r + rows, :]
        if copy_ref is not None:
            copy_ref[r:r + rows, :] = x
        ms = jnp.mean(x * x, axis=-1, keepdims=True)
        hb = (x * lax.rsqrt(ms + EPS) * gain + shift).astype(h_ref.dtype)
        h_ref[r:r + rows, :] = hb
        blocks.append(hb)
    return jnp.concatenate(blocks, axis=0)


def _mod_kernel(c_ref, w_ref, b_ref, o_ref):
    s = _silu(c_ref[...])
    w = w_ref[...]
    s1 = s.astype(BF16)
    s2 = (s - s1.astype(F32)).astype(BF16)
    w1 = w.astype(BF16)
    w2 = (w - w1.astype(F32)).astype(BF16)
    o_ref[...] = (_dot(s1, w1) + _dot(s2, w1) + _dot(s1, w2)) + b_ref[...]


def _mod_call(cond, w_mod, b_mod):
    depth, d, n = w_mod.shape
    rows = cond.shape[0]
    tn = _tile(n, 1024)
    return pl.pallas_call(
        _mod_kernel,
        grid=(depth, n // tn),
        in_specs=[
            pl.BlockSpec((rows, d), lambda l, j: (0, 0)),
            pl.BlockSpec((None, d, tn), lambda l, j: (l, 0, j)),
            pl.BlockSpec((None, 1, tn), lambda l, j: (l, 0, j)),
        ],
        out_specs=pl.BlockSpec((None, rows, tn), lambda l, j: (l, 0, j)),
        out_shape=jax.ShapeDtypeStruct((depth, rows, n), F32),
        compiler_params=_params("parallel", "parallel"),
        name="mod",
    )(cond, w_mod, b_mod.reshape(depth, 1, n))


def _in_kernel(x_ref, mod_ref, nw_ref, w_ref, *rest, with_dt):
    if with_dt:
        wdtT_ref, o_ref, dtT_ref, h_ref = rest
    else:
        o_ref, h_ref = rest

    j = pl.program_id(1)

    @pl.when(j == 0)
    def _():
        h = _norm_mod_rows(x_ref, nw_ref, mod_ref[0:1, :], mod_ref[1:2, :], h_ref)
        o_ref[...] = _dot(h, w_ref[...]).astype(o_ref.dtype)
        if with_dt:
            dtT_ref[...] = lax.dot_general(wdtT_ref[...], h, (((1,), (1,)), ((), ())),
                                           preferred_element_type=F32)

    @pl.when(j > 0)
    def _():
        o_ref[...] = _dot(h_ref[...], w_ref[...]).astype(o_ref.dtype)


def _in_call(x, mods, norm_w, w, layer, wl, n, seq_len, tn_pref, wdtT=None):
    t, d = x.shape
    tm, cid = _row_tile(t, seq_len, mods.shape[1], 1024)
    tn = _tile(n, tn_pref)
    with_dt = wdtT is not None
    in_specs = [
        pl.BlockSpec((tm, d), lambda i, j: (i, 0)),
        pl.BlockSpec((None, None, N_MOD, d), lambda i, j: (layer, cid(i), 0, 0)),
        pl.BlockSpec((None, 1, d), lambda i, j: (layer, 0, 0)),
        pl.BlockSpec((None, d, tn), lambda i, j: (wl, 0, j)),
    ]
    out_specs = [pl.BlockSpec((tm, tn), lambda i, j: (i, j))]
    out_shape = [jax.ShapeDtypeStruct((t, n), BF16)]
    args = [x, mods, norm_w, w]
    if with_dt:
        h2 = wdtT.shape[1]
        in_specs.append(pl.BlockSpec((None, h2, d), lambda i, j: (wl, 0, 0)))
        out_specs.append(pl.BlockSpec((h2, tm), lambda i, j: (0, i)))
        out_shape.append(jax.ShapeDtypeStruct((h2, t), F32))
        args.append(wdtT)
    return pl.pallas_call(
        functools.partial(_in_kernel, with_dt=with_dt),
        grid=(t // tm, n // tn),
        in_specs=in_specs,
        out_specs=out_specs,
        out_shape=out_shape,
        scratch_shapes=[pltpu.VMEM((tm, d), BF16)],
        compiler_params=_params("parallel", "arbitrary"),
        name="in_proj",
    )(*args)


def _gelu_tanh(x):
    k = math.sqrt(2.0 / math.pi)
    a = -2.0 * k * 0.044715 * LOG2E
    b = -2.0 * k * LOG2E
    return x / (1.0 + jnp.exp2(x * (x * x * a + b)))


def _cast_blocks(rows, nsteps):
    for nb in range(min(nsteps, rows), 0, -1):
        if rows % nb == 0 and (rows // nb) % 16 == 0:
            return nb
    return None


def _can_cast(casts, nsteps):
    for w, _ in casts:
        nb = _cast_blocks(w.shape[1], nsteps)
        if nb is None or w.shape[1] // nb * w.shape[2] * 4 > CAST_BLOCK_BYTES:
            return False
    return True


def _cast_plan(casts, nsteps, step_of):
    in_specs, out_specs, out_shape, args = [], [], [], []
    for w, wl in casts:
        rows, cols = w.shape[1:]
        nb = _cast_blocks(rows, nsteps)

        def blk(*g, nb=nb):
            return jnp.minimum(step_of(*g), nb - 1)

        in_specs.append(pl.BlockSpec((None, rows // nb, cols), lambda *g, blk=blk, wl=wl: (wl, blk(*g), 0)))
        out_specs.append(pl.BlockSpec((rows // nb, cols), lambda *g, blk=blk: (blk(*g), 0)))
        out_shape.append(jax.ShapeDtypeStruct((rows, cols), BF16))
        args.append(w)
    return in_specs, out_specs, out_shape, args


def _conv_kernel(cur_ref, prev_ref, next_ref, w_ref, b_ref, *rest, nc, n_cast):
    cast_in, o_ref, cast_out = rest[:n_cast], rest[n_cast], rest[n_cast + 1:]
    for src, dst in zip(cast_in, cast_out):
        dst[...] = src[...].astype(dst.dtype)

    c = pl.program_id(1)
    bq, cc = cur_ref.shape
    q = min(bq, SSD_CHUNK)
    rows = q + 2 * HALO
    ri = lax.broadcasted_iota(I32, (q, rows), 0)
    ci = lax.broadcasted_iota(I32, (q, rows), 1)
    picks = [None if k == D_CONV // 2 else
             jnp.where(ci == ri + (HALO + k - D_CONV // 2), 1.0, 0.0).astype(BF16) for k in range(D_CONV)]
    cw = min(CONV_COLS, cc)
    nsub = bq // q
    for n0 in range(0, cc, cw):
        cs = slice(n0, n0 + cw)
        prev = jnp.where(c > 0, prev_ref[:, cs], jnp.zeros((HALO, cw), BF16))
        nxt = jnp.where(c < nc - 1, next_ref[:, cs], jnp.zeros((HALO, cw), BF16))
        for u in range(nsub):
            cur = cur_ref[u * q:(u + 1) * q, cs]
            before = prev if u == 0 else cur_ref[u * q - HALO:u * q, cs]
            after = nxt if u == nsub - 1 else cur_ref[(u + 1) * q:(u + 1) * q + HALO, cs]
            ext = jnp.concatenate([before, cur, after], axis=0)
            acc = b_ref[:, cs]
            for k in range(D_CONV):
                tap = cur.astype(F32) if picks[k] is None else _dot(picks[k], ext)
                acc = acc + w_ref[k:k + 1, cs] * tap
            o_ref[u * q:(u + 1) * q, cs] = _silu(acc).astype(o_ref.dtype)


def _conv_steps(t, seq_len):
    q = _tile(seq_len, 2 * SSD_CHUNK)
    return q, (t // seq_len) * (seq_len // q)


def _conv_call(proj, conv_w, conv_b, layer, seq_len, col_block, cast=None):
    t = proj.shape[0]
    cc = conv_w.shape[2]
    q, nsteps = _conv_steps(t, seq_len)
    nc = seq_len // q
    nseq = t // seq_len
    per = q // HALO
    last = t // HALO - 1
    in_specs = [
        pl.BlockSpec((q, cc), lambda s, c: (s * nc + c, col_block)),
        pl.BlockSpec((HALO, cc), lambda s, c: (jnp.maximum((s * nc + c) * per - 1, 0), col_block)),
        pl.BlockSpec((HALO, cc), lambda s, c: (jnp.minimum((s * nc + c + 1) * per, last), col_block)),
        pl.BlockSpec((None, D_CONV, cc), lambda s, c: (layer, 0, 0)),
        pl.BlockSpec((None, 1, cc), lambda s, c: (layer, 0, 0)),
    ]
    args = [proj, proj, proj, conv_w, conv_b]
    out_specs = [pl.BlockSpec((q, cc), lambda s, c: (s * nc + c, 0))]
    out_shape = [jax.ShapeDtypeStruct((t, cc), BF16)]
    cast = cast or []
    extra = _cast_plan(cast, nsteps, lambda s, c: s * nc + c)
    in_specs += extra[0]
    out_specs += extra[1]
    out_shape += extra[2]
    args += extra[3]
    n_cast = len(cast)
    res = pl.pallas_call(
        functools.partial(_conv_kernel, nc=nc, n_cast=n_cast),
        grid=(nseq, nc),
        in_specs=in_specs,
        out_specs=out_specs,
        out_shape=out_shape,
        compiler_params=_params("arbitrary", "arbitrary"),
        name="conv",
    )(*args)
    return res[0], tuple(res[1:])


def _ssd_factors(dtT_ref, lanes, dtbT_ref, alogT_ref, rev, heads):
    q = lanes.stop - lanes.start

    def dsel(fwd, bwd):
        return jnp.where(rev, bwd, fwd)

    dtT = _softplus(dsel(dtT_ref[0:heads, lanes], dtT_ref[heads:2 * heads, lanes])
                    + dsel(dtbT_ref[0:heads, :], dtbT_ref[heads:2 * heads, :]))
    aT = dtT * (-jnp.exp(dsel(alogT_ref[0:heads, :], alogT_ref[heads:2 * heads, :]))) * LOG2E
    ri = lax.broadcasted_iota(I32, (q, q), 0)
    ci = lax.broadcasted_iota(I32, (q, q), 1)
    triT = dsel(jnp.where(ci >= ri, 1.0, 0.0), jnp.where(ri >= ci, 1.0, 0.0)).astype(BF16)
    b1, b2, b3 = _split3(aT)
    a_csT = (_dot(b1, triT) + _dot(b2, triT)) + _dot(b3, triT)
    a_cs = a_csT.T
    tot = dsel(a_cs[q - 1:q, :], a_cs[0:1, :])
    stack = jnp.concatenate([dtT.T, jnp.exp2(a_cs), jnp.exp2(tot - a_cs)], axis=0)
    s_hi = stack.astype(BF16)
    s_lo = (stack - s_hi.astype(F32)).astype(BF16)
    return a_cs, a_csT, jnp.concatenate([s_hi, s_lo], axis=1)


def _ssd_kernel(xbc_ref, dtT_ref, dtTn_ref, dtbT_ref, alogT_ref, dsk_ref,
                *rest, nc, heads, hdim, groups, nstate, has_h0, want_state, n_cast):
    rest = list(rest)
    h0_ref = so_ref = None
    if has_h0:
        h0_ref = rest.pop(0)
    cast_in = [rest.pop(0) for _ in range(n_cast)]
    y_ref = rest.pop(0)
    if want_state:
        so_ref = rest.pop(0)
    cast_out = [rest.pop(0) for _ in range(n_cast)]
    st_ref, yf_ref, acs_ref, acsT_ref, scat_ref = rest

    for src, dst in zip(cast_in, cast_out):
        dst[...] = src[...].astype(dst.dtype)

    q = SSD_CHUNK
    bq = xbc_ref.shape[0]
    nsub = bq // q
    db = heads * hdim
    gn = groups * nstate
    hpg = heads // groups
    gw = hpg * hdim
    seq_len = nc * bq
    t = pl.program_id(1)
    rev = t >= nc
    c = jnp.where(rev, 2 * nc - 1 - t, t)
    slot = t & 1

    def dsel(fwd, bwd):
        return jnp.where(rev, bwd, fwd)

    def stage(src_ref, is_rev, dst):
        for s in range(nsub):
            a_cs, a_csT, s_cat = _ssd_factors(src_ref, slice(s * q, (s + 1) * q), dtbT_ref, alogT_ref, is_rev,
                                              heads)
            acs_ref[dst, s] = a_cs
            acsT_ref[dst, s] = a_csT
            scat_ref[dst, s] = s_cat

    @pl.when(t == 0)
    def _():
        stage(dtT_ref, rev, slot)

    @pl.when((t == 0) | (t == nc))
    def _():
        st_ref[...] = h0_ref[...] if has_h0 else jnp.zeros_like(st_ref)

    ri = lax.broadcasted_iota(I32, (q, q), 0)
    ci = lax.broadcasted_iota(I32, (q, q), 1)
    mask = dsel(jnp.where(ri >= ci, 1.0, 0.0), jnp.where(ci >= ri, 1.0, 0.0)) > 0.5

    assert hdim & (hdim - 1) == 0 and heads & (heads - 1) == 0
    eh = lax.broadcasted_iota(I32, (2 * heads, db), 0) & (heads - 1)
    ec = lax.shift_right_logical(lax.broadcasted_iota(I32, (2 * heads, db), 1), hdim.bit_length() - 1)
    expand = jnp.where(eh == ec, 1.0, 0.0).astype(BF16)
    low_half = (lax.broadcasted_iota(I32, (q, gw), 1) & hdim) == 0

    for u in range(nsub):
        su = jnp.where(rev, nsub - 1 - u, u) if nsub > 1 else 0
        rows = pl.ds(pl.multiple_of(su * q, q), q) if nsub > 1 else slice(None)
        row0 = pl.multiple_of(c * bq + su * q, q)
        dst0 = pl.multiple_of(jnp.where(rev, seq_len, row0), q)
        a_cs = acs_ref[slot, su]
        a_csT = acsT_ref[slot, su]
        s_cat = scat_ref[slot, su]

        for g in range(groups):
            gc = slice(g * gw, (g + 1) * gw)
            ex = _dot(s_cat, expand[:, gc])
            dt_x, e_x, dec_x = ex[0:q], ex[q:2 * q], ex[2 * q:3 * q]
            xg = xbc_ref[rows, gc].astype(F32) * dt_x
            xg_lo = jnp.where(low_half, xg, 0.0).astype(BF16)
            xg_hi = jnp.where(low_half, 0.0, xg).astype(BF16)
            xdb = (xg * dec_x).astype(BF16)
            bg = xbc_ref[rows, db + g * nstate:db + (g + 1) * nstate]
            cg = xbc_ref[rows, db + gn + g * nstate:db + gn + (g + 1) * nstate]
            cb = lax.dot_general(cg, bg, (((1,), (1,)), ((), ())), preferred_element_type=F32)
            yo = _dot(cg, st_ref[:, gc].astype(BF16)) * e_x
            for pr in range(hpg // 2):
                h0 = g * hpg + 2 * pr
                ls = []
                for h in (h0, h0 + 1):
                    d = a_cs[:, h:h + 1] - a_csT[h:h + 1, :]
                    ls.append(jnp.where(mask, jnp.exp2(d), 0.0) * cb)
                lhs = jnp.concatenate(ls, axis=1).astype(BF16)
                pc = slice(2 * pr * hdim, (2 * pr + 2) * hdim)
                rhs = jnp.concatenate([xg_lo[:, pc], xg_hi[:, pc]], axis=0)
                yp = _dot(lhs, rhs) + yo[:, pc]
                cols = slice(h0 * hdim, (h0 + 2) * hdim)
                yf_ref[pl.ds(dst0, q), cols] = yp.astype(yf_ref.dtype)
                y_ref[rows, cols] = ((yf_ref[pl.ds(row0, q), cols].astype(F32) + yp)
                                     + dsk_ref[:, cols] * xbc_ref[rows, cols].astype(F32)).astype(y_ref.dtype)
            new_state = lax.dot_general(bg, xdb, (((0,), (0,)), ((), ())), preferred_element_type=F32)
            st_ref[:, gc] = st_ref[:, gc] * dsel(e_x[q - 1:q, :], e_x[0:1, :]) + new_state

    stage(dtTn_ref, t + 1 >= nc, 1 - slot)

    if want_state:
        @pl.when((t == nc - 1) | (t == 2 * nc - 1))
        def _():
            so_ref[...] = st_ref[...].T


def _ssd_steps(t, seq_len):
    bq = _tile(seq_len, SSD_STEP_CHUNKS * SSD_CHUNK)
    return bq, (t // seq_len) * 2 * (seq_len // bq)


def _ssd_call(xbc, dtT, dt_biasT, a_logT, d_skip_x, layer, seq_len, heads, hdim, nstate, h0T=None, state_out=None,
              cast=None):
    t, cc = xbc.shape
    q = SSD_CHUNK
    bq, nsteps = _ssd_steps(t, seq_len)
    nsub = bq // q
    nc = seq_len // bq
    nseq = t // seq_len
    db = heads * hdim
    h2 = 2 * heads
    has_h0 = h0T is not None
    want_state = state_out is not None
    n_even = dt_biasT.shape[0]

    def chunk(tt):
        return jnp.where(tt >= nc, 2 * nc - 1 - tt, tt)

    def late(tt):
        return jnp.where(tt >= nc, 2 * nc - 1 - tt, nc - 1)

    def direction(tt):
        return jnp.where(tt >= nc, 1, 0)

    in_specs = [
        pl.BlockSpec((bq, cc), lambda s, tt: (s * nc + chunk(tt), 0)),
        pl.BlockSpec((h2, bq), lambda s, tt: (0, s * nc + chunk(tt))),
        pl.BlockSpec((h2, bq), lambda s, tt: (0, s * nc + chunk(jnp.minimum(tt + 1, 2 * nc - 1)))),
        pl.BlockSpec((None, h2, 1), lambda s, tt: (layer, 0, 0)),
        pl.BlockSpec((None, h2, 1), lambda s, tt: (layer, 0, 0)),
        pl.BlockSpec((None, 1, db), lambda s, tt: (layer, 0, 0)),
    ]
    args = [xbc, dtT, dtT, dt_biasT, a_logT, d_skip_x]
    if has_h0:
        in_specs.append(pl.BlockSpec((None, None, None, nstate, db),
                                     lambda s, tt: (s, layer, direction(tt), 0, 0)))
        args.append(h0T)
    cast = cast or []
    extra = _cast_plan(cast, nsteps, lambda s, tt: s * (2 * nc) + tt)
    in_specs += extra[0]
    args += extra[3]
    out_specs = [pl.BlockSpec((bq, db), lambda s, tt: (s * nc + late(tt), 0))]
    out_shape = [jax.ShapeDtypeStruct((t, db), BF16)]
    aliases = {}
    if want_state:
        out_specs.append(pl.BlockSpec((None, None, None, db, nstate),
                                      lambda s, tt: (s, layer, direction(tt), 0, 0)))
        out_shape.append(jax.ShapeDtypeStruct((nseq, n_even, 2, db, nstate), F32))
        in_specs.append(pl.BlockSpec(memory_space=pl.ANY))
        args.append(state_out)
        aliases = {len(args) - 1: 1}
    out_specs += extra[1]
    out_shape += extra[2]
    kern = functools.partial(_ssd_kernel, nc=nc, heads=heads, hdim=hdim, groups=SSD_GROUPS, nstate=nstate,
                             has_h0=has_h0, want_state=want_state, n_cast=len(cast))
    if aliases:
        inner = kern
        n_in = len(args)

        def kern(*refs):
            return inner(*refs[:n_in - 1], *refs[n_in:])

    return pl.pallas_call(
        kern,
        grid=(nseq, 2 * nc),
        in_specs=in_specs,
        out_specs=out_specs,
        out_shape=out_shape,
        input_output_aliases=aliases,
        scratch_shapes=[
            pltpu.VMEM((nstate, db), F32),
            pltpu.VMEM((seq_len + q, db), BF16),
            pltpu.VMEM((2, nsub, q, heads), F32),
            pltpu.VMEM((2, nsub, heads, q), F32),
            pltpu.VMEM((2, nsub, 3 * q, h2), BF16),
        ],
        compiler_params=_params("arbitrary", "arbitrary"),
        name="ssd",
    )(*args)


def _mix_even_kernel(u_ref, v_ref, z_ref, ys_ref, ws_ref, bsT_ref, vn_ref, nw_ref, w_ref, x_ref, mod_ref, o_ref,
                     b_ref, *, groups):
    tm, d_a = u_ref.shape
    d_b = ys_ref.shape[1]
    ch = d_a // groups
    assert d_a == d_b
    gz = ys_ref[...].astype(F32) * _silu(z_ref[...].astype(F32))
    ms = jnp.mean(gz * gz, axis=-1, keepdims=True)
    b_ref[...] = (gz * lax.rsqrt(ms + EPS) * nw_ref[...]).astype(BF16)
    acc = None
    for g in range(groups):
        sl = slice(g * ch, (g + 1) * ch)
        part = _dot(b_ref[:, sl], w_ref[d_a + g * ch:d_a + (g + 1) * ch, :])
        acc = part if acc is None else acc + part
        vns = []
        for c0 in range(0, tm, CHUNK):
            v = _gelu_tanh(v_ref[c0:c0 + CHUNK, sl].astype(F32))
            ms = jnp.mean(v * v, axis=-1, keepdims=True)
            vns.append((v * lax.rsqrt(ms + EPS) * vn_ref[:, sl]).astype(BF16))
        sv = _dot(ws_ref[g], jnp.concatenate(vns, axis=1)) + bsT_ref[:, g:g + 1]
        parts = [(_gelu_tanh(u_ref[c * CHUNK:(c + 1) * CHUNK, sl].astype(F32)) * sv[:, c * ch:(c + 1) * ch])
                 .astype(BF16) for c in range(tm // CHUNK)]
        acc = acc + _dot(jnp.concatenate(parts, axis=0), w_ref[sl, :])
    o_ref[...] = x_ref[...] + mod_ref[2:3, :] * acc


def _mix_even_call(proj, ysum, z_block, ws, bsT, vnorm, norm_w, w, x, mods, layer, wl, w_idx, seq_len):
    t, d = x.shape
    d_b = ysum.shape[1]
    d_a = w.shape[1] - d_b
    groups = ws.shape[1]
    assert d_a == d_b
    tm, cid = _row_tile(t, seq_len, mods.shape[1], 2 * CHUNK)
    return pl.pallas_call(
        functools.partial(_mix_even_kernel, groups=groups),
        grid=(t // tm,),
        in_specs=[
            pl.BlockSpec((tm, d_a), lambda i: (i, 0)),
            pl.BlockSpec((tm, d_a), lambda i: (i, 1)),
            pl.BlockSpec((tm, d_b), lambda i: (i, z_block)),
            pl.BlockSpec((tm, d_b), lambda i: (i, 0)),
            pl.BlockSpec((None, groups, CHUNK, CHUNK), lambda i: (wl, 0, 0, 0)),
            pl.BlockSpec((None, CHUNK, groups), lambda i: (wl, 0, 0)),
            pl.BlockSpec((None, 1, d_a), lambda i: (wl, 0, 0)),
            pl.BlockSpec((None, 1, d_b), lambda i: (wl, 0, 0)),
            pl.BlockSpec((None, d_a + d_b, d), lambda i: (w_idx, 0, 0), pipeline_mode=pl.Buffered(1)),
            pl.BlockSpec((tm, d), lambda i: (i, 0)),
            pl.BlockSpec((None, None, N_MOD, d), lambda i: (layer, cid(i), 0, 0)),
        ],
        out_specs=pl.BlockSpec((tm, d), lambda i: (i, 0)),
        out_shape=jax.ShapeDtypeStruct((t, d), F32),
        scratch_shapes=[pltpu.VMEM((tm, d_b), BF16)],
        compiler_params=_params("parallel"),
        name="mix_even",
    )(proj, proj, proj, ysum, ws, bsT, vnorm, norm_w, w, x, mods)


def _pool_kernel(hc_ref, pw_ref, ps_ref, wo_ref, x_ref, mod_ref, o_ref, *, win, sub):
    tm, d_c = hc_ref.shape
    ng = len(POOL_WINDOWS)
    pch = d_c // ng
    assert win & (win - 1) == 0
    ri = lax.broadcasted_iota(I32, (sub, sub), 0)
    ci = lax.broadcasted_iota(I32, (sub, sub), 1)
    same = lax.shift_right_logical(ri, win.bit_length() - 1) == lax.shift_right_logical(ci, win.bit_length() - 1)
    pos = lax.broadcasted_iota(I32, (sub, 1), 0) & (win - 1)
    mixed = []
    for g, k in enumerate(POOL_WINDOWS):
        off = ci - ri
        band = jnp.where(same & (off >= -(k // 2)) & (off < k - k // 2), 1.0, 0.0).astype(BF16)
        lo = jnp.maximum(pos - k // 2, 0)
        hi = jnp.minimum(pos - k // 2 + k, win)
        cnt = (hi - lo).astype(F32)
        pooled = []
        for r in range(tm // sub):
            xb = hc_ref[r * sub:(r + 1) * sub, g * pch:(g + 1) * pch]
            pooled.append(_dot(band, xb) / cnt - xb.astype(F32))
        pg = jnp.concatenate(pooled, axis=0).astype(BF16)
        mixed.append(_dot(pg, pw_ref[g]) * ps_ref[:, g * pch:(g + 1) * pch])
    mix = jnp.concatenate(mixed, axis=1).astype(BF16)
    o_ref[...] = x_ref[...] + mod_ref[2:3, :] * _dot(mix, wo_ref[...])


def _pool_call(hc, pool_w, pool_scale, w_out, x, mods, layer, wl, seq_len, win):
    t, d = x.shape
    d_c = hc.shape[1]
    _, ng, pch, _ = pool_w.shape
    sub = max(win, 128)
    assert sub % win == 0 and seq_len % sub == 0
    tm, cid = _row_tile(t, seq_len, mods.shape[1], 512)
    assert tm % sub == 0
    return pl.pallas_call(
        functools.partial(_pool_kernel, win=win, sub=sub),
        grid=(t // tm,),
        in_specs=[
            pl.BlockSpec((tm, d_c), lambda i: (i, 0)),
            pl.BlockSpec((None, ng, pch, pch), lambda i: (wl, 0, 0, 0)),
            pl.BlockSpec((None, 1, d_c), lambda i: (wl, 0, 0)),
            pl.BlockSpec((None, d_c, d), lambda i: (wl, 0, 0)),
            pl.BlockSpec((tm, d), lambda i: (i, 0)),
            pl.BlockSpec((None, None, N_MOD, d), lambda i: (layer, cid(i), 0, 0)),
        ],
        out_specs=pl.BlockSpec((tm, d), lambda i: (i, 0)),
        out_shape=jax.ShapeDtypeStruct((t, d), F32),
        compiler_params=_params("parallel"),
        name="pool_out",
    )(hc, pool_w, pool_scale, w_out, x, mods)


def _ffn_kernel(x_ref, mod_ref, nw_ref, w1_ref, w3_ref, w2_ref, fn_ref, o_ref, h_ref, *, nf, final, dn):
    f = pl.program_id(1)
    tm, d = o_ref.shape

    def step(h, last):
        act = (_silu(_dot(h, w1_ref[...])) * _dot(h, w3_ref[...])).astype(BF16)
        if not (final and last):
            for n0 in range(0, d, dn):
                o_ref[:, n0:n0 + dn] += mod_ref[5:6, n0:n0 + dn] * _dot(act, w2_ref[:, n0:n0 + dn])
            return
        rb = min(FINAL_NORM_ROWS, tm)
        for r0 in range(0, tm, rb):
            y = o_ref[r0:r0 + rb, :] + mod_ref[5:6, :] * _dot(act[r0:r0 + rb], w2_ref[...])
            ms = jnp.mean(y * y, axis=-1, keepdims=True)
            o_ref[r0:r0 + rb, :] = y * lax.rsqrt(ms + EPS) * fn_ref[...]

    def run(first, last):
        if first:
            step(_norm_mod_rows(x_ref, nw_ref, mod_ref[3:4, :], mod_ref[4:5, :], h_ref, copy_ref=o_ref), last)
        else:
            step(h_ref[...], last)

    if nf == 1:
        run(True, True)
    elif not final:
        pl.when(f == 0)(lambda: run(True, False))
        pl.when(f > 0)(lambda: run(False, False))
    else:
        pl.when(f == 0)(lambda: run(True, False))
        pl.when(f == nf - 1)(lambda: run(False, True))
        if nf > 2:
            pl.when((f > 0) & (f < nf - 1))(lambda: run(False, False))


def _ffn_call(x, mods, norm_w, w, final_nw, layer, seq_len, final):
    t, d = x.shape
    w1, w3, w2 = w
    dff = w1.shape[1]
    tm, cid = _row_tile(t, seq_len, mods.shape[1], 1024)
    tf = _tile(dff, 512)
    nf = dff // tf
    return pl.pallas_call(
        functools.partial(_ffn_kernel, nf=nf, final=final, dn=_tile(d, 512)),
        grid=(t // tm, nf),
        in_specs=[
            pl.BlockSpec((tm, d), lambda i, f: (i, 0)),
            pl.BlockSpec((None, None, N_MOD, d), lambda i, f: (layer, cid(i), 0, 0)),
            pl.BlockSpec((None, 1, d), lambda i, f: (layer, 0, 0)),
            pl.BlockSpec((d, tf), lambda i, f: (0, f)),
            pl.BlockSpec((d, tf), lambda i, f: (0, f)),
            pl.BlockSpec((tf, d), lambda i, f: (f, 0)),
            pl.BlockSpec((1, d), lambda i, f: (0, 0)),
        ],
        out_specs=pl.BlockSpec((tm, d), lambda i, f: (i, 0)),
        out_shape=jax.ShapeDtypeStruct((t, d), F32),
        scratch_shapes=[pltpu.VMEM((tm, d), BF16)],
        compiler_params=_params("parallel", "arbitrary"),
        name="ffn",
    )(x, mods, norm_w, w1, w3, w2, final_nw)


def _mixer_layer(x, i, seq_len, mods, win, h0T, state, p, cast_conv=None, cast_ssd=None):
    by_conv = by_ssd = ()
    if i % 2 == 0:
        e = i // 2
        w_out = p["w_out_even"][e]
        proj, dtT = _in_call(x, mods, p["norm_mix"], p["w_in_even"], i, e, p["n_main"], seq_len, 2304,
                             p["w_in_dtT"])
        xbc, by_conv = _conv_call(proj, p["ssd_conv_w"], p["ssd_conv_b"], e, seq_len, p["xbc_block"], cast_conv)
        res = _ssd_call(xbc, dtT, p["ssd_dt_biasT"], p["ssd_a_logT"], p["ssd_d_x"], e, seq_len,
                        p["heads"], p["hdim"], p["nstate"], h0T, state, cast_ssd)
        if state is not None:
            state = res[1]
        by_ssd = tuple(res[1 if state is None else 2:])
        x = _mix_even_call(proj, res[0], p["z_block"], p["gmlp_ws"], p["gmlp_bsT"], p["gmlp_norm"],
                           p["ssd_norm"], w_out, x, mods, i, e, 0, seq_len)
    else:
        o = i // 2
        (hc,) = _in_call(x, mods, p["norm_mix"], p["w_in_odd"], i, o, p["w_in_odd"].shape[2], seq_len, 2048)
        x = _pool_call(hc, p["pool_w"], p["pool_scale"], p["w_out_odd"], x, mods, i, o, seq_len, win)
    return x, state, by_conv, by_ssd


def kernel(x_prompt, x_sample, state_ssd, c, c_ctx, w_mod, b_mod, norm_mix, norm_ffn, w_in_even, w_out_even, gmlp_norm, gmlp_ws, gmlp_bs, ssd_conv_w, ssd_conv_b, ssd_dt_bias, ssd_a_log, ssd_d, ssd_norm, w_in_odd, pool_w, pool_scale, w_out_odd, ffn_w1, ffn_w3, ffn_w2, final_norm):
    batch, seq, d = x_prompt.shape
    dec_batch, dec_seq, _ = x_sample.shape
    depth = w_mod.shape[0]
    n_even = w_in_even.shape[0]
    heads, hdim, nstate = state_ssd.shape[3:]
    d_b = heads * hdim
    d_a = gmlp_norm.shape[1]
    cc = ssd_conv_w.shape[2]
    n_main = 2 * d_a + d_b + cc
    h2 = 2 * heads
    assert w_in_even.shape[2] == n_main + h2 and h2 <= LANES
    assert (2 * d_a) % d_b == 0 and (n_main - cc) % cc == 0

    n_cond = 1 + dec_batch
    rows = -(-n_cond // 8) * 8
    cond = jnp.concatenate([c_ctx[None], c, jnp.zeros((rows - n_cond, d), F32)], axis=0)
    mods = _mod_call(cond, w_mod, b_mod).reshape(depth, rows, N_MOD, d)

    p = dict(
        heads=heads, hdim=hdim, nstate=nstate, n_main=n_main,
        z_block=(2 * d_a) // d_b, xbc_block=(n_main - cc) // cc,
        norm_mix=norm_mix.reshape(depth, 1, d),
        w_in_even=w_in_even.astype(BF16),
        w_in_dtT=jnp.swapaxes(lax.optimization_barrier(w_in_even[:, :, n_main:]), 1, 2).astype(BF16),
        gmlp_norm=gmlp_norm.reshape(n_even, 1, d_a), gmlp_ws=gmlp_ws.astype(BF16),
        gmlp_bsT=jnp.swapaxes(gmlp_bs, 1, 2),
        ssd_conv_w=ssd_conv_w, ssd_conv_b=ssd_conv_b.reshape(n_even, 1, cc),
        ssd_dt_biasT=ssd_dt_bias.reshape(n_even, h2, 1), ssd_a_logT=ssd_a_log.reshape(n_even, h2, 1),
        ssd_d_x=jnp.repeat(ssd_d, hdim, axis=1).reshape(n_even, 1, d_b),
        ssd_norm=ssd_norm.reshape(n_even, 1, d_b),
        pool_scale=pool_scale.reshape(-1, 1, d),
    )
    norm_ffn = norm_ffn.reshape(depth, 1, d)
    final_norm = final_norm.reshape(1, d)

    x_p = x_prompt.reshape(batch * seq, d)
    x_s = x_sample.reshape(dec_batch * dec_seq, d)
    mods_p, mods_s = mods[:, 0:1], mods[:, 1:n_cond]
    h0T = jnp.swapaxes(state_ssd.reshape(dec_batch, n_even, 2, d_b, nstate), 3, 4)
    state_new = jnp.zeros((batch, n_even, 2, d_b, nstate), F32)

    ffn_f32 = (ffn_w1, ffn_w3, ffn_w2)
    ffn_bf16 = {}
    first_ffn = [(w, 0) for w in ffn_f32]
    host_first = _can_cast(first_ffn, _ssd_steps(x_p.shape[0], seq)[1])
    p["w_out_even"] = {0: w_out_even[0:1].astype(BF16)}
    odd_f32 = (w_in_odd, w_out_odd, pool_w)
    late = [(w.reshape(1, -1, w.shape[-1]), 0) for w in odd_f32]
    late += [(w_out_even, e) for e in range(1, n_even)]
    host_late = depth > 1 and _can_cast(late, _ssd_steps(x_s.shape[0], dec_seq)[1])
    if not host_late:
        p["w_in_odd"], p["w_out_odd"], p["pool_w"] = (w.astype(BF16) for w in odd_f32)
        for e in range(1, n_even):
            p["w_out_even"][e] = w_out_even[e:e + 1].astype(BF16)

    for i in range(depth):
        pending = [j for j in range(i + 1, depth) if j not in ffn_bf16]
        cast_s = cast_p = None
        if i % 2 == 0:
            if pending and _can_cast([(w, 0) for w in ffn_f32], _conv_steps(x_s.shape[0], dec_seq)[1]):
                nxt = pending.pop(0)
                cast_s = [(w, nxt) for w in ffn_f32]
            if pending and _can_cast([(w, 0) for w in ffn_f32], _conv_steps(x_p.shape[0], seq)[1]):
                nxt = pending.pop(0)
                cast_p = [(w, nxt) for w in ffn_f32]
        x_p, state_new, conv_p, ssd_p = _mixer_layer(x_p, i, seq, mods_p, seq, None, state_new, p, cast_p,
                                                     first_ffn if host_first and i == 0 else None)
        x_s, _, conv_s, ssd_s = _mixer_layer(x_s, i, dec_seq, mods_s, GRID_W, h0T, None, p, cast_s,
                                             late if host_late and i == 0 else None)
        if host_first and i == 0:
            ffn_bf16[0] = ssd_p
        if cast_p is not None:
            ffn_bf16[cast_p[0][1]] = conv_p
        if cast_s is not None:
            ffn_bf16[cast_s[0][1]] = conv_s
        if host_late and i == 0:
            p["w_in_odd"], p["w_out_odd"], p["pool_w"] = (c.reshape(w.shape) for c, w in zip(ssd_s, odd_f32))
            for e in range(1, n_even):
                p["w_out_even"][e] = ssd_s[len(odd_f32) + e - 1][None]
        if i not in ffn_bf16:
            ffn_bf16[i] = tuple(w[i].astype(BF16) for w in ffn_f32)
        last = i == depth - 1
        x_p = _ffn_call(x_p, mods_p, norm_ffn, ffn_bf16[i], final_norm, i, seq, last)
        x_s = _ffn_call(x_s, mods_s, norm_ffn, ffn_bf16[i], final_norm, i, dec_seq, last)

    return (x_p.reshape(batch, seq, d), x_s.reshape(dec_batch, dec_seq, d),
            state_new.reshape(batch, n_even, 2, heads, hdim, nstate))
```

```python
import functools
import math

import jax
import jax.numpy as jnp
from jax import lax
from jax.experimental import pallas as pl
from jax.experimental.pallas import tpu as pltpu

F32 = jnp.float32
BF16 = jnp.bfloat16
I32 = jnp.int32

EPS = 1e-6
N_MOD = 6
GRID_W = 64
CHUNK = 128
SSD_CHUNK = 128
SSD_STEP_CHUNKS = 4
SSD_GROUPS = 4
D_CONV = 5
POOL_WINDOWS = (2, 4, 8, 16)
LANES = 128
HALO = 16
NORM_ROWS = 32
FINAL_NORM_ROWS = 256
CONV_COLS = 512
CAST_BLOCK_BYTES = 2 * 1024 * 1024
LOG2E = 1.4426950408889634
VMEM_LIMIT_BYTES = 56 * 1024 * 1024


def _params(*sem):
    return pltpu.CompilerParams(dimension_semantics=sem, vmem_limit_bytes=VMEM_LIMIT_BYTES)


def _tile(n, pref):
    t = min(n, pref)
    assert n % t == 0, (n, pref)
    return t


def _row_tile(t, seq_len, n_cond, pref):
    if n_cond == 1:
        return _tile(t, pref), (lambda i: 0)
    tm = _tile(seq_len, min(pref, seq_len))
    return tm, (lambda i: (i * tm) // seq_len)


def _silu(x):
    return x * jax.nn.sigmoid(x)


def _softplus(x):
    return jnp.maximum(x, 0.0) + jnp.log1p(jnp.exp(-jnp.abs(x)))


def _split3(x):
    x1 = x.astype(BF16)
    r = x - x1.astype(F32)
    x2 = r.astype(BF16)
    x3 = (r - x2.astype(F32)).astype(BF16)
    return x1, x2, x3


def _dot(a, b):
    return jnp.dot(a, b, preferred_element_type=F32)


def _norm_mod_rows(x_ref, nw_ref, shift, scale, h_ref):
    gain = nw_ref[...] * (1.0 + scale)
    tm = x_ref.shape[0]
    rows = min(NORM_ROWS, tm)
    blocks = []
    for r in range(0, tm, rows):
        x = x_ref[r:r + rows, :]
        ms = jnp.mean(x * x, axis=-1, keepdims=True)
        hb = (x * lax.rsqrt(ms + EPS) * gain + shift).astype(h_ref.dtype)
        h_ref[r:r + rows, :] = hb
        blocks.append(hb)
    return jnp.concatenate(blocks, axis=0)


def _mod_kernel(c_ref, w_ref, b_ref, o_ref):
    s = _silu(c_ref[...])
    w = w_ref[...]
    s1 = s.astype(BF16)
    s2 = (s - s1.astype(F32)).astype(BF16)
    w1 = w.astype(BF16)
    w2 = (w - w1.astype(F32)).astype(BF16)
    o_ref[...] = (_dot(s1, w1) + _dot(s2, w1) + _dot(s1, w2)) + b_ref[...]


def _mod_call(cond, w_mod, b_mod):
    depth, d, n = w_mod.shape
    rows = cond.shape[0]
    tn = _tile(n, 1024)
    return pl.pallas_call(
        _mod_kernel,
        grid=(depth, n // tn),
        in_specs=[
            pl.BlockSpec((rows, d), lambda l, j: (0, 0)),
            pl.BlockSpec((None, d, tn), lambda l, j: (l, 0, j)),
            pl.BlockSpec((None, 1, tn), lambda l, j: (l, 0, j)),
        ],
        out_specs=pl.BlockSpec((None, rows, tn), lambda l, j: (l, 0, j)),
        out_shape=jax.ShapeDtypeStruct((depth, rows, n), F32),
        compiler_params=_params("parallel", "parallel"),
        name="mod",
    )(cond, w_mod, b_mod.reshape(depth, 1, n))


def _in_kernel(x_ref, mod_ref, nw_ref, w_ref, *rest, with_dt):
    if with_dt:
        wdtT_ref, o_ref, dtT_ref, h_ref = rest
    else:
        o_ref, h_ref = rest

    j = pl.program_id(1)

    @pl.when(j == 0)
    def _():
        h = _norm_mod_rows(x_ref, nw_ref, mod_ref[0:1, :], mod_ref[1:2, :], h_ref)
        o_ref[...] = _dot(h, w_ref[...]).astype(o_ref.dtype)
        if with_dt:
            dtT_ref[...] = lax.dot_general(wdtT_ref[...], h, (((1,), (1,)), ((), ())),
                                           preferred_element_type=F32)

    @pl.when(j > 0)
    def _():
        o_ref[...] = _dot(h_ref[...], w_ref[...]).astype(o_ref.dtype)


def _in_call(x, mods, norm_w, w, layer, wl, n, seq_len, tn_pref, wdtT=None):
    t, d = x.shape
    tm, cid = _row_tile(t, seq_len, mods.shape[1], 1024)
    tn = _tile(n, tn_pref)
    with_dt = wdtT is not None
    in_specs = [
        pl.BlockSpec((tm, d), lambda i, j: (i, 0)),
        pl.BlockSpec((None, None, N_MOD, d), lambda i, j: (layer, cid(i), 0, 0)),
        pl.BlockSpec((None, 1, d), lambda i, j: (layer, 0, 0)),
        pl.BlockSpec((None, d, tn), lambda i, j: (wl, 0, j)),
    ]
    out_specs = [pl.BlockSpec((tm, tn), lambda i, j: (i, j))]
    out_shape = [jax.ShapeDtypeStruct((t, n), BF16)]
    args = [x, mods, norm_w, w]
    if with_dt:
        h2 = wdtT.shape[1]
        in_specs.append(pl.BlockSpec((None, h2, d), lambda i, j: (wl, 0, 0)))
        out_specs.append(pl.BlockSpec((h2, tm), lambda i, j: (0, i)))
        out_shape.append(jax.ShapeDtypeStruct((h2, t), F32))
        args.append(wdtT)
    return pl.pallas_call(
        functools.partial(_in_kernel, with_dt=with_dt),
        grid=(t // tm, n // tn),
        in_specs=in_specs,
        out_specs=out_specs,
        out_shape=out_shape,
        scratch_shapes=[pltpu.VMEM((tm, d), BF16)],
        compiler_params=_params("parallel", "arbitrary"),
        name="in_proj",
    )(*args)


def _gelu_tanh(x):
    k = math.sqrt(2.0 / math.pi)
    a = -2.0 * k * 0.044715 * LOG2E
    b = -2.0 * k * LOG2E
    return x / (1.0 + jnp.exp2(x * (x * x * a + b)))


def _cast_blocks(rows, nsteps):
    for nb in range(min(nsteps, rows), 0, -1):
        if rows % nb == 0 and (rows // nb) % 16 == 0:
            return nb
    return None


def _can_cast(casts, nsteps):
    for w, _ in casts:
        nb = _cast_blocks(w.shape[1], nsteps)
        if nb is None or w.shape[1] // nb * w.shape[2] * 4 > CAST_BLOCK_BYTES:
            return False
    return True


def _cast_plan(casts, nsteps, step_of):
    in_specs, out_specs, out_shape, args = [], [], [], []
    for w, wl in casts:
        rows, cols = w.shape[1:]
        nb = _cast_blocks(rows, nsteps)

        def blk(*g, nb=nb):
            return jnp.minimum(step_of(*g), nb - 1)

        in_specs.append(pl.BlockSpec((None, rows // nb, cols), lambda *g, blk=blk, wl=wl: (wl, blk(*g), 0)))
        out_specs.append(pl.BlockSpec((rows // nb, cols), lambda *g, blk=blk: (blk(*g), 0)))
        out_shape.append(jax.ShapeDtypeStruct((rows, cols), BF16))
        args.append(w)
    return in_specs, out_specs, out_shape, args


def _conv_kernel(cur_ref, prev_ref, next_ref, w_ref, b_ref, *rest, nc, n_cast):
    cast_in, o_ref, cast_out = rest[:n_cast], rest[n_cast], rest[n_cast + 1:]
    for src, dst in zip(cast_in, cast_out):
        dst[...] = src[...].astype(dst.dtype)

    c = pl.program_id(1)
    bq, cc = cur_ref.shape
    q = min(bq, SSD_CHUNK)
    rows = q + 2 * HALO
    ri = lax.broadcasted_iota(I32, (q, rows), 0)
    ci = lax.broadcasted_iota(I32, (q, rows), 1)
    picks = [None if k == D_CONV // 2 else
             jnp.where(ci == ri + (HALO + k - D_CONV // 2), 1.0, 0.0).astype(BF16) for k in range(D_CONV)]
    cw = min(CONV_COLS, cc)
    nsub = bq // q
    for n0 in range(0, cc, cw):
        cs = slice(n0, n0 + cw)
        prev = jnp.where(c > 0, prev_ref[:, cs], jnp.zeros((HALO, cw), BF16))
        nxt = jnp.where(c < nc - 1, next_ref[:, cs], jnp.zeros((HALO, cw), BF16))
        for u in range(nsub):
            cur = cur_ref[u * q:(u + 1) * q, cs]
            before = prev if u == 0 else cur_ref[u * q - HALO:u * q, cs]
            after = nxt if u == nsub - 1 else cur_ref[(u + 1) * q:(u + 1) * q + HALO, cs]
            ext = jnp.concatenate([before, cur, after], axis=0)
            acc = b_ref[:, cs]
            for k in range(D_CONV):
                tap = cur.astype(F32) if picks[k] is None else _dot(picks[k], ext)
                acc = acc + w_ref[k:k + 1, cs] * tap
            o_ref[u * q:(u + 1) * q, cs] = _silu(acc).astype(o_ref.dtype)


def _conv_steps(t, seq_len):
    q = _tile(seq_len, 2 * SSD_CHUNK)
    return q, (t // seq_len) * (seq_len // q)


def _conv_call(proj, conv_w, conv_b, layer, seq_len, col_block, cast=None):
    t = proj.shape[0]
    cc = conv_w.shape[2]
    q, nsteps = _conv_steps(t, seq_len)
    nc = seq_len // q
    nseq = t // seq_len
    per = q // HALO
    last = t // HALO - 1
    in_specs = [
        pl.BlockSpec((q, cc), lambda s, c: (s * nc + c, col_block)),
        pl.BlockSpec((HALO, cc), lambda s, c: (jnp.maximum((s * nc + c) * per - 1, 0), col_block)),
        pl.BlockSpec((HALO, cc), lambda s, c: (jnp.minimum((s * nc + c + 1) * per, last), col_block)),
        pl.BlockSpec((None, D_CONV, cc), lambda s, c: (layer, 0, 0)),
        pl.BlockSpec((None, 1, cc), lambda s, c: (layer, 0, 0)),
    ]
    args = [proj, proj, proj, conv_w, conv_b]
    out_specs = [pl.BlockSpec((q, cc), lambda s, c: (s * nc + c, 0))]
    out_shape = [jax.ShapeDtypeStruct((t, cc), BF16)]
    cast = cast or []
    extra = _cast_plan(cast, nsteps, lambda s, c: s * nc + c)
    in_specs += extra[0]
    out_specs += extra[1]
    out_shape += extra[2]
    args += extra[3]
    n_cast = len(cast)
    res = pl.pallas_call(
        functools.partial(_conv_kernel, nc=nc, n_cast=n_cast),
        grid=(nseq, nc),
        in_specs=in_specs,
        out_specs=out_specs,
        out_shape=out_shape,
        compiler_params=_params("arbitrary", "arbitrary"),
        name="conv",
    )(*args)
    return res[0], tuple(res[1:])


def _ssd_factors(dtT_ref, lanes, dtbT_ref, alogT_ref, rev, heads):
    q = lanes.stop - lanes.start

    def dsel(fwd, bwd):
        return jnp.where(rev, bwd, fwd)

    dtT = _softplus(dsel(dtT_ref[0:heads, lanes], dtT_ref[heads:2 * heads, lanes])
                    + dsel(dtbT_ref[0:heads, :], dtbT_ref[heads:2 * heads, :]))
    aT = dtT * (-jnp.exp(dsel(alogT_ref[0:heads, :], alogT_ref[heads:2 * heads, :]))) * LOG2E
    ri = lax.broadcasted_iota(I32, (q, q), 0)
    ci = lax.broadcasted_iota(I32, (q, q), 1)
    triT = dsel(jnp.where(ci >= ri, 1.0, 0.0), jnp.where(ri >= ci, 1.0, 0.0)).astype(BF16)
    b1, b2, b3 = _split3(aT)
    a_csT = (_dot(b1, triT) + _dot(b2, triT)) + _dot(b3, triT)
    a_cs = a_csT.T
    tot = dsel(a_cs[q - 1:q, :], a_cs[0:1, :])
    stack = jnp.concatenate([dtT.T, jnp.exp2(a_cs), jnp.exp2(tot - a_cs)], axis=0)
    s_hi = stack.astype(BF16)
    s_lo = (stack - s_hi.astype(F32)).astype(BF16)
    return a_cs, a_csT, jnp.concatenate([s_hi, s_lo], axis=1)


def _ssd_kernel(xbc_ref, dtT_ref, dtTn_ref, dtbT_ref, alogT_ref, dsk_ref,
                *rest, nc, heads, hdim, groups, nstate, has_h0, want_state, n_cast):
    rest = list(rest)
    h0_ref = so_ref = None
    if has_h0:
        h0_ref = rest.pop(0)
    cast_in = [rest.pop(0) for _ in range(n_cast)]
    y_ref = rest.pop(0)
    if want_state:
        so_ref = rest.pop(0)
    cast_out = [rest.pop(0) for _ in range(n_cast)]
    st_ref, yf_ref, acs_ref, acsT_ref, scat_ref = rest

    for src, dst in zip(cast_in, cast_out):
        dst[...] = src[...].astype(dst.dtype)

    q = SSD_CHUNK
    bq = xbc_ref.shape[0]
    nsub = bq // q
    db = heads * hdim
    gn = groups * nstate
    hpg = heads // groups
    gw = hpg * hdim
    seq_len = nc * bq
    t = pl.program_id(1)
    rev = t >= nc
    c = jnp.where(rev, 2 * nc - 1 - t, t)
    slot = t & 1

    def dsel(fwd, bwd):
        return jnp.where(rev, bwd, fwd)

    def stage(src_ref, is_rev, dst):
        for s in range(nsub):
            a_cs, a_csT, s_cat = _ssd_factors(src_ref, slice(s * q, (s + 1) * q), dtbT_ref, alogT_ref, is_rev,
                                              heads)
            acs_ref[dst, s] = a_cs
            acsT_ref[dst, s] = a_csT
            scat_ref[dst, s] = s_cat

    @pl.when(t == 0)
    def _():
        stage(dtT_ref, rev, slot)

    @pl.when((t == 0) | (t == nc))
    def _():
        st_ref[...] = h0_ref[...] if has_h0 else jnp.zeros_like(st_ref)

    ri = lax.broadcasted_iota(I32, (q, q), 0)
    ci = lax.broadcasted_iota(I32, (q, q), 1)
    mask = dsel(jnp.where(ri >= ci, 1.0, 0.0), jnp.where(ci >= ri, 1.0, 0.0)) > 0.5

    assert hdim & (hdim - 1) == 0 and heads & (heads - 1) == 0
    eh = lax.broadcasted_iota(I32, (2 * heads, db), 0) & (heads - 1)
    ec = lax.shift_right_logical(lax.broadcasted_iota(I32, (2 * heads, db), 1), hdim.bit_length() - 1)
    expand = jnp.where(eh == ec, 1.0, 0.0).astype(BF16)
    low_half = (lax.broadcasted_iota(I32, (q, gw), 1) & hdim) == 0

    for u in range(nsub):
        su = jnp.where(rev, nsub - 1 - u, u) if nsub > 1 else 0
        rows = pl.ds(pl.multiple_of(su * q, q), q) if nsub > 1 else slice(None)
        row0 = pl.multiple_of(c * bq + su * q, q)
        dst0 = pl.multiple_of(jnp.where(rev, seq_len, row0), q)
        a_cs = acs_ref[slot, su]
        a_csT = acsT_ref[slot, su]
        s_cat = scat_ref[slot, su]

        for g in range(groups):
            gc = slice(g * gw, (g + 1) * gw)
            ex = _dot(s_cat, expand[:, gc])
            dt_x, e_x, dec_x = ex[0:q], ex[q:2 * q], ex[2 * q:3 * q]
            xg = xbc_ref[rows, gc].astype(F32) * dt_x
            xg_lo = jnp.where(low_half, xg, 0.0).astype(BF16)
            xg_hi = jnp.where(low_half, 0.0, xg).astype(BF16)
            xdb = (xg * dec_x).astype(BF16)
            bg = xbc_ref[rows, db + g * nstate:db + (g + 1) * nstate]
            cg = xbc_ref[rows, db + gn + g * nstate:db + gn + (g + 1) * nstate]
            cb = lax.dot_general(cg, bg, (((1,), (1,)), ((), ())), preferred_element_type=F32)
            yo = _dot(cg, st_ref[:, gc].astype(BF16)) * e_x
            for pr in range(hpg // 2):
                h0 = g * hpg + 2 * pr
                ls = []
                for h in (h0, h0 + 1):
                    d = a_cs[:, h:h + 1] - a_csT[h:h + 1, :]
                    ls.append(jnp.where(mask, jnp.exp2(d), 0.0) * cb)
                lhs = jnp.concatenate(ls, axis=1).astype(BF16)
                pc = slice(2 * pr * hdim, (2 * pr + 2) * hdim)
                rhs = jnp.concatenate([xg_lo[:, pc], xg_hi[:, pc]], axis=0)
                yp = _dot(lhs, rhs) + yo[:, pc]
                cols = slice(h0 * hdim, (h0 + 2) * hdim)
                yf_ref[pl.ds(dst0, q), cols] = yp.astype(yf_ref.dtype)
                y_ref[rows, cols] = ((yf_ref[pl.ds(row0, q), cols].astype(F32) + yp)
                                     + dsk_ref[:, cols] * xbc_ref[rows, cols].astype(F32)).astype(y_ref.dtype)
            new_state = lax.dot_general(bg, xdb, (((0,), (0,)), ((), ())), preferred_element_type=F32)
            st_ref[:, gc] = st_ref[:, gc] * dsel(e_x[q - 1:q, :], e_x[0:1, :]) + new_state

    stage(dtTn_ref, t + 1 >= nc, 1 - slot)

    if want_state:
        @pl.when((t == nc - 1) | (t == 2 * nc - 1))
        def _():
            so_ref[...] = st_ref[...].T


def _ssd_steps(t, seq_len):
    bq = _tile(seq_len, SSD_STEP_CHUNKS * SSD_CHUNK)
    return bq, (t // seq_len) * 2 * (seq_len // bq)


def _ssd_call(xbc, dtT, dt_biasT, a_logT, d_skip_x, layer, seq_len, heads, hdim, nstate, h0T=None, state_out=None,
              cast=None):
    t, cc = xbc.shape
    q = SSD_CHUNK
    bq, nsteps = _ssd_steps(t, seq_len)
    nsub = bq // q
    nc = seq_len // bq
    nseq = t // seq_len
    db = heads * hdim
    h2 = 2 * heads
    has_h0 = h0T is not None
    want_state = state_out is not None
    n_even = dt_biasT.shape[0]

    def chunk(tt):
        return jnp.where(tt >= nc, 2 * nc - 1 - tt, tt)

    def late(tt):
        return jnp.where(tt >= nc, 2 * nc - 1 - tt, nc - 1)

    def direction(tt):
        return jnp.where(tt >= nc, 1, 0)

    in_specs = [
        pl.BlockSpec((bq, cc), lambda s, tt: (s * nc + chunk(tt), 0)),
        pl.BlockSpec((h2, bq), lambda s, tt: (0, s * nc + chunk(tt))),
        pl.BlockSpec((h2, bq), lambda s, tt: (0, s * nc + chunk(jnp.minimum(tt + 1, 2 * nc - 1)))),
        pl.BlockSpec((None, h2, 1), lambda s, tt: (layer, 0, 0)),
        pl.BlockSpec((None, h2, 1), lambda s, tt: (layer, 0, 0)),
        pl.BlockSpec((None, 1, db), lambda s, tt: (layer, 0, 0)),
    ]
    args = [xbc, dtT, dtT, dt_biasT, a_logT, d_skip_x]
    if has_h0:
        in_specs.append(pl.BlockSpec((None, None, None, nstate, db),
                                     lambda s, tt: (s, layer, direction(tt), 0, 0)))
        args.append(h0T)
    cast = cast or []
    extra = _cast_plan(cast, nsteps, lambda s, tt: s * (2 * nc) + tt)
    in_specs += extra[0]
    args += extra[3]
    out_specs = [pl.BlockSpec((bq, db), lambda s, tt: (s * nc + late(tt), 0))]
    out_shape = [jax.ShapeDtypeStruct((t, db), BF16)]
    aliases = {}
    if want_state:
        out_specs.append(pl.BlockSpec((None, None, None, db, nstate),
                                      lambda s, tt: (s, layer, direction(tt), 0, 0)))
        out_shape.append(jax.ShapeDtypeStruct((nseq, n_even, 2, db, nstate), F32))
        in_specs.append(pl.BlockSpec(memory_space=pl.ANY))
        args.append(state_out)
        aliases = {len(args) - 1: 1}
    out_specs += extra[1]
    out_shape += extra[2]
    kern = functools.partial(_ssd_kernel, nc=nc, heads=heads, hdim=hdim, groups=SSD_GROUPS, nstate=nstate,
                             has_h0=has_h0, want_state=want_state, n_cast=len(cast))
    if aliases:
        inner = kern
        n_in = len(args)

        def kern(*refs):
            return inner(*refs[:n_in - 1], *refs[n_in:])

    return pl.pallas_call(
        kern,
        grid=(nseq, 2 * nc),
        in_specs=in_specs,
        out_specs=out_specs,
        out_shape=out_shape,
        input_output_aliases=aliases,
        scratch_shapes=[
            pltpu.VMEM((nstate, db), F32),
            pltpu.VMEM((seq_len + q, db), BF16),
            pltpu.VMEM((2, nsub, q, heads), F32),
            pltpu.VMEM((2, nsub, heads, q), F32),
            pltpu.VMEM((2, nsub, 3 * q, h2), BF16),
        ],
        compiler_params=_params("arbitrary", "arbitrary"),
        name="ssd",
    )(*args)


def _mix_even_kernel(u_ref, v_ref, z_ref, ys_ref, ws_ref, bsT_ref, vn_ref, nw_ref, w_ref, x_ref, mod_ref, o_ref,
                     b_ref, *, groups):
    tm, d_a = u_ref.shape
    d_b = ys_ref.shape[1]
    ch = d_a // groups
    assert d_a == d_b
    gz = ys_ref[...].astype(F32) * _silu(z_ref[...].astype(F32))
    ms = jnp.mean(gz * gz, axis=-1, keepdims=True)
    b_ref[...] = (gz * lax.rsqrt(ms + EPS) * nw_ref[...]).astype(BF16)
    acc = None
    for g in range(groups):
        sl = slice(g * ch, (g + 1) * ch)
        part = _dot(b_ref[:, sl], w_ref[d_a + g * ch:d_a + (g + 1) * ch, :])
        acc = part if acc is None else acc + part
        vns = []
        for c0 in range(0, tm, CHUNK):
            v = _gelu_tanh(v_ref[c0:c0 + CHUNK, sl].astype(F32))
            ms = jnp.mean(v * v, axis=-1, keepdims=True)
            vns.append((v * lax.rsqrt(ms + EPS) * vn_ref[:, sl]).astype(BF16))
        sv = _dot(ws_ref[g], jnp.concatenate(vns, axis=1)) + bsT_ref[:, g:g + 1]
        parts = [(_gelu_tanh(u_ref[c * CHUNK:(c + 1) * CHUNK, sl].astype(F32)) * sv[:, c * ch:(c + 1) * ch])
                 .astype(BF16) for c in range(tm // CHUNK)]
        acc = acc + _dot(jnp.concatenate(parts, axis=0), w_ref[sl, :])
    o_ref[...] = x_ref[...] + mod_ref[2:3, :] * acc


def _mix_even_call(proj, ysum, z_block, ws, bsT, vnorm, norm_w, w, x, mods, layer, wl, w_idx, seq_len):
    t, d = x.shape
    d_b = ysum.shape[1]
    d_a = w.shape[1] - d_b
    groups = ws.shape[1]
    assert d_a == d_b
    tm, cid = _row_tile(t, seq_len, mods.shape[1], 2 * CHUNK)
    return pl.pallas_call(
        functools.partial(_mix_even_kernel, groups=groups),
        grid=(t // tm,),
        in_specs=[
            pl.BlockSpec((tm, d_a), lambda i: (i, 0)),
            pl.BlockSpec((tm, d_a), lambda i: (i, 1)),
            pl.BlockSpec((tm, d_b), lambda i: (i, z_block)),
            pl.BlockSpec((tm, d_b), lambda i: (i, 0)),
            pl.BlockSpec((None, groups, CHUNK, CHUNK), lambda i: (wl, 0, 0, 0)),
            pl.BlockSpec((None, CHUNK, groups), lambda i: (wl, 0, 0)),
            pl.BlockSpec((None, 1, d_a), lambda i: (wl, 0, 0)),
            pl.BlockSpec((None, 1, d_b), lambda i: (wl, 0, 0)),
            pl.BlockSpec((None, d_a + d_b, d), lambda i: (w_idx, 0, 0), pipeline_mode=pl.Buffered(1)),
            pl.BlockSpec((tm, d), lambda i: (i, 0)),
            pl.BlockSpec((None, None, N_MOD, d), lambda i: (layer, cid(i), 0, 0)),
        ],
        out_specs=pl.BlockSpec((tm, d), lambda i: (i, 0)),
        out_shape=jax.ShapeDtypeStruct((t, d), F32),
        scratch_shapes=[pltpu.VMEM((tm, d_b), BF16)],
        compiler_params=_params("parallel"),
        name="mix_even",
    )(proj, proj, proj, ysum, ws, bsT, vnorm, norm_w, w, x, mods)


def _pool_kernel(hc_ref, pw_ref, ps_ref, wo_ref, x_ref, mod_ref, o_ref, *, win, sub):
    tm, d_c = hc_ref.shape
    ng = len(POOL_WINDOWS)
    pch = d_c // ng
    assert win & (win - 1) == 0
    ri = lax.broadcasted_iota(I32, (sub, sub), 0)
    ci = lax.broadcasted_iota(I32, (sub, sub), 1)
    same = lax.shift_right_logical(ri, win.bit_length() - 1) == lax.shift_right_logical(ci, win.bit_length() - 1)
    pos = lax.broadcasted_iota(I32, (sub, 1), 0) & (win - 1)
    mixed = []
    for g, k in enumerate(POOL_WINDOWS):
        off = ci - ri
        band = jnp.where(same & (off >= -(k // 2)) & (off < k - k // 2), 1.0, 0.0).astype(BF16)
        lo = jnp.maximum(pos - k // 2, 0)
        hi = jnp.minimum(pos - k // 2 + k, win)
        cnt = (hi - lo).astype(F32)
        pooled = []
        for r in range(tm // sub):
            xb = hc_ref[r * sub:(r + 1) * sub, g * pch:(g + 1) * pch]
            pooled.append(_dot(band, xb) / cnt - xb.astype(F32))
        pg = jnp.concatenate(pooled, axis=0).astype(BF16)
        mixed.append(_dot(pg, pw_ref[g]) * ps_ref[:, g * pch:(g + 1) * pch])
    mix = jnp.concatenate(mixed, axis=1).astype(BF16)
    o_ref[...] = x_ref[...] + mod_ref[2:3, :] * _dot(mix, wo_ref[...])


def _pool_call(hc, pool_w, pool_scale, w_out, x, mods, layer, wl, seq_len, win):
    t, d = x.shape
    d_c = hc.shape[1]
    _, ng, pch, _ = pool_w.shape
    sub = max(win, 128)
    assert sub % win == 0 and seq_len % sub == 0
    tm, cid = _row_tile(t, seq_len, mods.shape[1], 512)
    assert tm % sub == 0
    return pl.pallas_call(
        functools.partial(_pool_kernel, win=win, sub=sub),
        grid=(t // tm,),
        in_specs=[
            pl.BlockSpec((tm, d_c), lambda i: (i, 0)),
            pl.BlockSpec((None, ng, pch, pch), lambda i: (wl, 0, 0, 0)),
            pl.BlockSpec((None, 1, d_c), lambda i: (wl, 0, 0)),
            pl.BlockSpec((None, d_c, d), lambda i: (wl, 0, 0)),
            pl.BlockSpec((tm, d), lambda i: (i, 0)),
            pl.BlockSpec((None, None, N_MOD, d), lambda i: (layer, cid(i), 0, 0)),
        ],
        out_specs=pl.BlockSpec((tm, d), lambda i: (i, 0)),
        out_shape=jax.ShapeDtypeStruct((t, d), F32),
        compiler_params=_params("parallel"),
        name="pool_out",
    )(hc, pool_w, pool_scale, w_out, x, mods)


def _ffn_kernel(x_ref, mod_ref, nw_ref, w1_ref, w3_ref, w2_ref, fn_ref, o_ref, h_ref, *, nf, final, dn):
    f = pl.program_id(1)
    tm, d = o_ref.shape

    def step(h, first, last):
        act = (_silu(_dot(h, w1_ref[...])) * _dot(h, w3_ref[...])).astype(BF16)
        base = x_ref if first else o_ref
        if not (final and last):
            for n0 in range(0, d, dn):
                o_ref[:, n0:n0 + dn] = (base[:, n0:n0 + dn]
                                        + mod_ref[5:6, n0:n0 + dn] * _dot(act, w2_ref[:, n0:n0 + dn]))
            return
        rb = min(FINAL_NORM_ROWS, tm)
        for r0 in range(0, tm, rb):
            y = base[r0:r0 + rb, :] + mod_ref[5:6, :] * _dot(act[r0:r0 + rb], w2_ref[...])
            ms = jnp.mean(y * y, axis=-1, keepdims=True)
            o_ref[r0:r0 + rb, :] = y * lax.rsqrt(ms + EPS) * fn_ref[...]

    def run(first, last):
        if first:
            step(_norm_mod_rows(x_ref, nw_ref, mod_ref[3:4, :], mod_ref[4:5, :], h_ref), True, last)
        else:
            step(h_ref[...], False, last)

    if nf == 1:
        run(True, True)
    elif not final:
        pl.when(f == 0)(lambda: run(True, False))
        pl.when(f > 0)(lambda: run(False, False))
    else:
        pl.when(f == 0)(lambda: run(True, False))
        pl.when(f == nf - 1)(lambda: run(False, True))
        if nf > 2:
            pl.when((f > 0) & (f < nf - 1))(lambda: run(False, False))


def _ffn_call(x, mods, norm_w, w, final_nw, layer, seq_len, final):
    t, d = x.shape
    w1, w3, w2 = w
    dff = w1.shape[1]
    tm, cid = _row_tile(t, seq_len, mods.shape[1], 1024)
    tf = _tile(dff, 512)
    nf = dff // tf
    return pl.pallas_call(
        functools.partial(_ffn_kernel, nf=nf, final=final, dn=_tile(d, 512)),
        grid=(t // tm, nf),
        in_specs=[
            pl.BlockSpec((tm, d), lambda i, f: (i, 0)),
            pl.BlockSpec((None, None, N_MOD, d), lambda i, f: (layer, cid(i), 0, 0)),
            pl.BlockSpec((None, 1, d), lambda i, f: (layer, 0, 0)),
            pl.BlockSpec((d, tf), lambda i, f: (0, f)),
            pl.BlockSpec((d, tf), lambda i, f: (0, f)),
            pl.BlockSpec((tf, d), lambda i, f: (f, 0)),
            pl.BlockSpec((1, d), lambda i, f: (0, 0)),
        ],
        out_specs=pl.BlockSpec((tm, d), lambda i, f: (i, 0)),
        out_shape=jax.ShapeDtypeStruct((t, d), F32),
        scratch_shapes=[pltpu.VMEM((tm, d), BF16)],
        compiler_params=_params("parallel", "arbitrary"),
        name="ffn",
    )(x, mods, norm_w, w1, w3, w2, final_nw)


def _mixer_layer(x, i, seq_len, mods, win, h0T, state, p, cast_conv=None, cast_ssd=None):
    by_conv = by_ssd = ()
    if i % 2 == 0:
        e = i // 2
        w_out = p["w_out_even"][e]
        proj, dtT = _in_call(x, mods, p["norm_mix"], p["w_in_even"], i, e, p["n_main"], seq_len, 2304,
                             p["w_in_dtT"])
        xbc, by_conv = _conv_call(proj, p["ssd_conv_w"], p["ssd_conv_b"], e, seq_len, p["xbc_block"], cast_conv)
        res = _ssd_call(xbc, dtT, p["ssd_dt_biasT"], p["ssd_a_logT"], p["ssd_d_x"], e, seq_len,
                        p["heads"], p["hdim"], p["nstate"], h0T, state, cast_ssd)
        if state is not None:
            state = res[1]
        by_ssd = tuple(res[1 if state is None else 2:])
        x = _mix_even_call(proj, res[0], p["z_block"], p["gmlp_ws"], p["gmlp_bsT"], p["gmlp_norm"],
                           p["ssd_norm"], w_out, x, mods, i, e, 0, seq_len)
    else:
        o = i // 2
        (hc,) = _in_call(x, mods, p["norm_mix"], p["w_in_odd"], i, o, p["w_in_odd"].shape[2], seq_len, 2048)
        x = _pool_call(hc, p["pool_w"], p["pool_scale"], p["w_out_odd"], x, mods, i, o, seq_len, win)
    return x, state, by_conv, by_ssd


def kernel(x_prompt, x_sample, state_ssd, c, c_ctx, w_mod, b_mod, norm_mix, norm_ffn, w_in_even, w_out_even, gmlp_norm, gmlp_ws, gmlp_bs, ssd_conv_w, ssd_conv_b, ssd_dt_bias, ssd_a_log, ssd_d, ssd_norm, w_in_odd, pool_w, pool_scale, w_out_odd, ffn_w1, ffn_w3, ffn_w2, final_norm):
    batch, seq, d = x_prompt.shape
    dec_batch, dec_seq, _ = x_sample.shape
    depth = w_mod.shape[0]
    n_even = w_in_even.shape[0]
    heads, hdim, nstate = state_ssd.shape[3:]
    d_b = heads * hdim
    d_a = gmlp_norm.shape[1]
    cc = ssd_conv_w.shape[2]
    n_main = 2 * d_a + d_b + cc
    h2 = 2 * heads
    assert w_in_even.shape[2] == n_main + h2 and h2 <= LANES
    assert (2 * d_a) % d_b == 0 and (n_main - cc) % cc == 0

    n_cond = 1 + dec_batch
    rows = -(-n_cond // 8) * 8
    cond = jnp.concatenate([c_ctx[None], c, jnp.zeros((rows - n_cond, d), F32)], axis=0)
    mods = _mod_call(cond, w_mod, b_mod).reshape(depth, rows, N_MOD, d)

    p = dict(
        heads=heads, hdim=hdim, nstate=nstate, n_main=n_main,
        z_block=(2 * d_a) // d_b, xbc_block=(n_main - cc) // cc,
        norm_mix=norm_mix.reshape(depth, 1, d),
        w_in_even=w_in_even.astype(BF16),
        w_in_dtT=jnp.swapaxes(lax.optimization_barrier(w_in_even[:, :, n_main:]), 1, 2).astype(BF16),
        gmlp_norm=gmlp_norm.reshape(n_even, 1, d_a), gmlp_ws=gmlp_ws.astype(BF16),
        gmlp_bsT=jnp.swapaxes(gmlp_bs, 1, 2),
        ssd_conv_w=ssd_conv_w, ssd_conv_b=ssd_conv_b.reshape(n_even, 1, cc),
        ssd_dt_biasT=ssd_dt_bias.reshape(n_even, h2, 1), ssd_a_logT=ssd_a_log.reshape(n_even, h2, 1),
        ssd_d_x=jnp.repeat(ssd_d, hdim, axis=1).reshape(n_even, 1, d_b),
        ssd_norm=ssd_norm.reshape(n_even, 1, d_b),
        pool_scale=pool_scale.reshape(-1, 1, d),
    )
    norm_ffn = norm_ffn.reshape(depth, 1, d)
    final_norm = final_norm.reshape(1, d)

    x_p = x_prompt.reshape(batch * seq, d)
    x_s = x_sample.reshape(dec_batch * dec_seq, d)
    mods_p, mods_s = mods[:, 0:1], mods[:, 1:n_cond]
    h0T = jnp.swapaxes(state_ssd.reshape(dec_batch, n_even, 2, d_b, nstate), 3, 4)
    state_new = jnp.zeros((batch, n_even, 2, d_b, nstate), F32)

    ffn_f32 = (ffn_w1, ffn_w3, ffn_w2)
    ffn_bf16 = {0: tuple(w[0].astype(BF16) for w in ffn_f32)}
    p["w_out_even"] = {0: w_out_even[0:1].astype(BF16)}
    odd_f32 = (w_in_odd, w_out_odd, pool_w)
    late = [(w.reshape(1, -1, w.shape[-1]), 0) for w in odd_f32]
    late += [(w_out_even, e) for e in range(1, n_even)]
    host_late = depth > 1 and _can_cast(late, _ssd_steps(x_s.shape[0], dec_seq)[1])
    if not host_late:
        p["w_in_odd"], p["w_out_odd"], p["pool_w"] = (w.astype(BF16) for w in odd_f32)
        for e in range(1, n_even):
            p["w_out_even"][e] = w_out_even[e:e + 1].astype(BF16)

    for i in range(depth):
        pending = [j for j in range(i + 1, depth) if j not in ffn_bf16]
        cast_s = cast_p = None
        if i % 2 == 0:
            if pending and _can_cast([(w, 0) for w in ffn_f32], _conv_steps(x_s.shape[0], dec_seq)[1]):
                nxt = pending.pop(0)
                cast_s = [(w, nxt) for w in ffn_f32]
            if pending and _can_cast([(w, 0) for w in ffn_f32], _conv_steps(x_p.shape[0], seq)[1]):
                nxt = pending.pop(0)
                cast_p = [(w, nxt) for w in ffn_f32]
        x_p, state_new, conv_p, _ = _mixer_layer(x_p, i, seq, mods_p, seq, None, state_new, p, cast_p)
        x_s, _, conv_s, ssd_s = _mixer_layer(x_s, i, dec_seq, mods_s, GRID_W, h0T, None, p, cast_s,
                                             late if host_late and i == 0 else None)
        if cast_p is not None:
            ffn_bf16[cast_p[0][1]] = conv_p
        if cast_s is not None:
            ffn_bf16[cast_s[0][1]] = conv_s
        if host_late and i == 0:
            p["w_in_odd"], p["w_out_odd"], p["pool_w"] = (c.reshape(w.shape) for c, w in zip(ssd_s, odd_f32))
            for e in range(1, n_even):
                p["w_out_even"][e] = ssd_s[len(odd_f32) + e - 1][None]
        if i not in ffn_bf16:
            ffn_bf16[i] = tuple(w[i].astype(BF16) for w in ffn_f32)
        last = i == depth - 1
        x_p = _ffn_call(x_p, mods_p, norm_ffn, ffn_bf16[i], final_norm, i, seq, last)
        x_s = _ffn_call(x_s, mods_s, norm_ffn, ffn_bf16[i], final_norm, i, dec_seq, last)

    return (x_p.reshape(batch, seq, d), x_s.reshape(dec_batch, dec_seq, d),
            state_new.reshape(batch, n_even, 2, heads, hdim, nstate))
```
